```python
import math, functools
import jax, jax.numpy as jnp
from jax import lax
import numpy as np

D_MODEL = 1024
BATCH = 4
SEQ = 4096
DEPTH = 4
DEC_BATCH = 128
DEC_SEQ = 8
PAST_LEN = 2048
PAGE_SIZE = 128

N_A = DEPTH // 2
N_B = DEPTH - N_A
D_FF = ((8 * D_MODEL // 3 + 127) // 128) * 128
P_DIM = 256
H_A = 8
DK = 128
DV = 128
QKV_DIM = H_A * (2 * DK + DV)
PROJ_A = QKV_DIM + H_A * DV + 2 * H_A
CONV_W = 4
CHUNK = 64
H_B = 16
HD_B = D_MODEL // H_B
Q_BLOCK = 128
SB_BIAS_INIT = -8.0
EPS = 1e-6

kernel_name = "yoco_gdn_stickbreaking_macaron_step"


def _rms_norm(x, g):
    xf = x.astype(jnp.float32)
    y = xf * lax.rsqrt(jnp.mean(xf * xf, axis=-1, keepdims=True) + EPS)
    return (y * g.astype(jnp.float32)).astype(x.dtype)


def _l2norm(x):
    return x * lax.rsqrt(jnp.sum(x * x, axis=-1, keepdims=True) + EPS)


def _swiglu(h, w_in, w_out):
    gu = h @ w_in
    gate, up = gu[..., :D_FF], gu[..., D_FF:]
    return (jax.nn.silu(gate) * up) @ w_out


def _gated_delta_chunked(q, k, v, g, beta, s0):
    f32 = jnp.float32
    bsz, t_len, h, dk = q.shape
    dv = v.shape[-1]
    c = min(CHUNK, t_len)
    pad = (-t_len) % c
    q = _l2norm(q.astype(f32)) * (dk ** -0.5)
    k = _l2norm(k.astype(f32))
    v = v.astype(f32)
    g = g.astype(f32)
    beta = beta.astype(f32)
    if pad:
        pw = ((0, 0), (0, pad), (0, 0), (0, 0))
        q, k, v = jnp.pad(q, pw), jnp.pad(k, pw), jnp.pad(v, pw)
        g, beta = jnp.pad(g, pw[:3]), jnp.pad(beta, pw[:3])
    n = (t_len + pad) // c

    def blk(t):
        return jnp.moveaxis(t.reshape((bsz, n, c, h) + t.shape[3:]), 3, 1)

    q, k, v, g, beta = blk(q), blk(k), blk(v), blk(g), blk(beta)
    gc = jnp.cumsum(g, axis=-1)
    incl = jnp.tril(jnp.ones((c, c), dtype=bool))
    strict = jnp.tril(jnp.ones((c, c), dtype=bool), -1)
    diff = gc[..., :, None] - gc[..., None, :]
    decay = jnp.where(incl, jnp.exp(jnp.where(incl, diff, 0.0)), 0.0)
    kb = k * beta[..., None]
    m = jnp.where(strict, jnp.einsum('bhnid,bhnjd->bhnij', kb, k) * decay, 0.0)
    eye = jnp.eye(c, dtype=f32)
    rhs = jnp.concatenate([v * beta[..., None], kb * jnp.exp(gc)[..., None]], axis=-1)
    sol = lax.linalg.triangular_solve(eye + m, rhs, left_side=True, lower=True,
                                      unit_diagonal=True)
    u, w = sol[..., :dv], sol[..., dv:]
    qk = jnp.where(incl, jnp.einsum('bhnid,bhnjd->bhnij', q, k) * decay, 0.0)
    q_dec = q * jnp.exp(gc)[..., None]
    k_dec = k * jnp.exp(gc[..., -1:] - gc)[..., None]
    g_tot = jnp.exp(gc[..., -1])

    def step(s, xs):
        u_n, w_n, qk_n, qd_n, kd_n, gt_n = xs
        v_new = u_n - jnp.einsum('bhck,bhkv->bhcv', w_n, s)
        o_n = (jnp.einsum('bhck,bhkv->bhcv', qd_n, s)
               + jnp.einsum('bhcs,bhsv->bhcv', qk_n, v_new))
        s = s * gt_n[..., None, None] + jnp.einsum('bhck,bhcv->bhkv', kd_n, v_new)
        return s, o_n

    xs = tuple(jnp.moveaxis(t, 2, 0) for t in (u, w, qk, q_dec, k_dec, g_tot))
    s_fin, o = lax.scan(step, s0.astype(f32), xs)
    o = jnp.moveaxis(jnp.moveaxis(o, 0, 2), 1, 3).reshape(bsz, n * c, h, dv)[:, :t_len]
    return o, s_fin


def _gdn_mixer(h, conv_prev, s0, w_in, w_conv, a_log, dt_bias, o_norm, w_out):
    bsz, t_len, _ = h.shape
    proj = h @ w_in
    qkv = proj[..., :QKV_DIM]
    gate = proj[..., QKV_DIM:QKV_DIM + H_A * DV]
    a_in = proj[..., QKV_DIM + H_A * DV:QKV_DIM + H_A * DV + H_A]
    b_in = proj[..., QKV_DIM + H_A * DV + H_A:]
    xp = jnp.concatenate([conv_prev.astype(qkv.dtype), qkv], axis=1)
    conv = xp[:, 0:t_len] * w_conv[0]
    for j in range(1, CONV_W):
        conv = conv + xp[:, j:j + t_len] * w_conv[j]
    conv = jax.nn.silu(conv)
    new_conv = xp[:, t_len:]
    q = conv[..., :H_A * DK].reshape(bsz, t_len, H_A, DK)
    k = conv[..., H_A * DK:2 * H_A * DK].reshape(bsz, t_len, H_A, DK)
    v = conv[..., 2 * H_A * DK:].reshape(bsz, t_len, H_A, DV)
    beta = jax.nn.sigmoid(b_in.astype(jnp.float32))
    g = -jnp.exp(a_log.astype(jnp.float32)) * jax.nn.softplus(
        a_in.astype(jnp.float32) + dt_bias.astype(jnp.float32))
    o, s_new = _gated_delta_chunked(q, k, v, g, beta, s0)
    o = _rms_norm(o.astype(h.dtype), o_norm) * jax.nn.silu(gate.reshape(bsz, t_len, H_A, DV))
    return o.reshape(bsz, t_len, H_A * DV) @ w_out, new_conv, s_new.astype(h.dtype)


def _stick_breaking(q, k, v, bias, q_start):
    f32 = jnp.float32
    t_q = q.shape[1]
    qb = min(Q_BLOCK, t_q)
    scale = HD_B ** -0.5
    bias_f = bias.astype(f32)[None, :, None, None]
    outs = []
    for s0 in range(0, t_q, qb):
        s1 = min(s0 + qb, t_q)
        n_keys = q_start + s1
        kb, vb = k[:, :n_keys], v[:, :n_keys]
        z = jnp.einsum('bqhd,bkhd->bhqk', q[:, s0:s1].astype(f32), kb.astype(f32)) * scale + bias_f
        q_pos = q_start + s0 + jnp.arange(s1 - s0)
        k_pos = jnp.arange(n_keys)
        mask = k_pos[None, :] < q_pos[:, None]
        log_keep = jnp.where(mask, jax.nn.log_sigmoid(-z), 0.0)
        after = lax.cumsum(log_keep, axis=3, reverse=True) - log_keep
        att = jnp.where(mask, jnp.exp(jax.nn.log_sigmoid(z) + after), 0.0)
        outs.append(jnp.einsum('bhqk,bkhd->bqhd', att.astype(v.dtype), vb))
    return jnp.concatenate(outs, axis=1)


def _sb_mixer(h, k_all, v_all, q_start, w_q, bias, w_o):
    bsz, t_len, _ = h.shape
    q = (h @ w_q).reshape(bsz, t_len, H_B, HD_B)
    o = _stick_breaking(q, k_all, v_all, bias, q_start)
    return o.reshape(bsz, t_len, H_B * HD_B) @ w_o


def _trunk(x, p, conv0, rec0, k_past, v_past, w):
    bsz, t_len, _ = x.shape
    q_start = 0 if k_past is None else k_past.shape[1]
    convs, recs = [], []
    k_all = v_all = k_new = v_new = None
    for i in range(DEPTH):
        x = x + 0.5 * _swiglu(_rms_norm(x, w['ffn1_norm'][i]), w['ffn1_w_in'][i], w['ffn1_w_out'][i])
        h = _rms_norm(x, w['mix_norm'][i])
        if i < N_A:
            y, c_new, s_new = _gdn_mixer(h, conv0[i], rec0[i], w['a_w_in'][i], w['a_conv'][i],
                                         w['a_A_log'][i], w['a_dt_bias'][i], w['a_o_norm'][i],
                                         w['a_w_out'][i])
            convs.append(c_new)
            recs.append(s_new)
        else:
            y = _sb_mixer(h, k_all, v_all, q_start, w['b_w_q'][i - N_A], w['b_bias'][i - N_A],
                          w['b_w_o'][i - N_A])
        x = x + y
        x = x + 0.5 * _swiglu(_rms_norm(x, w['ffn2_norm'][i]), w['ffn2_w_in'][i], w['ffn2_w_out'][i])
        x = x + (p[i] @ w['ple_w_in'][i]) * jax.nn.sigmoid(
            _rms_norm(x, w['ple_norm'][i]) @ w['ple_w_gate'][i])
        if i == N_A - 1:
            kv = _rms_norm(x, w['kv_norm']) @ w['w_kv']
            k_new = kv[..., :H_B * HD_B].reshape(bsz, t_len, H_B, HD_B)
            v_new = kv[..., H_B * HD_B:].reshape(bsz, t_len, H_B, HD_B)
            if k_past is None:
                k_all, v_all = k_new, v_new
            else:
                k_all = jnp.concatenate([k_past.astype(k_new.dtype), k_new], axis=1)
                v_all = jnp.concatenate([v_past.astype(v_new.dtype), v_new], axis=1)
    y = _rms_norm(x, w['final_norm'])
    return y, jnp.stack(convs), jnp.stack(recs), k_new, v_new


def setup_inputs(seed: int = 0) -> dict:
    key = jax.random.key(seed)
    keys = iter(jax.random.split(key, 48))
    f32 = jnp.float32

    def nrm(shape, scale=1.0):
        return jax.random.normal(next(keys), shape, f32) * scale

    def gain(shape):
        return 1.0 + 0.02 * jax.random.normal(next(keys), shape, f32)

    n_pages = PAST_LEN // PAGE_SIZE
    n_used = DEC_BATCH * n_pages
    n_phys = n_used + max(1, n_used // 4)
    page_table = jax.random.permutation(next(keys), n_phys)[:n_used].reshape(
        DEC_BATCH, n_pages).astype(jnp.int32)
    dt = jax.random.uniform(next(keys), (N_A, H_A), f32, minval=0.001, maxval=0.1)
    return {
        'x_prompt': nrm((BATCH, SEQ, D_MODEL)),
        'x_sample': nrm((DEC_BATCH, DEC_SEQ, D_MODEL)),
        'cache_k': nrm((n_phys, PAGE_SIZE, H_B, HD_B)),
        'cache_v': nrm((n_phys, PAGE_SIZE, H_B, HD_B)),
        'state_conv': nrm((N_A, DEC_BATCH, CONV_W - 1, QKV_DIM)),
        'state_rec': nrm((N_A, DEC_BATCH, H_A, DK, DV), 0.1),
        'page_table': page_table,
        'p_prompt': nrm((DEPTH, BATCH, SEQ, P_DIM)),
        'p_sample': nrm((DEPTH, DEC_BATCH, DEC_SEQ, P_DIM)),
        'ffn1_norm': gain((DEPTH, D_MODEL)),
        'ffn1_w_in': nrm((DEPTH, D_MODEL, 2 * D_FF), D_MODEL ** -0.5),
        'ffn1_w_out': nrm((DEPTH, D_FF, D_MODEL), D_FF ** -0.5),
        'mix_norm': gain((DEPTH, D_MODEL)),
        'ffn2_norm': gain((DEPTH, D_MODEL)),
        'ffn2_w_in': nrm((DEPTH, D_MODEL, 2 * D_FF), D_MODEL ** -0.5),
        'ffn2_w_out': nrm((DEPTH, D_FF, D_MODEL), D_FF ** -0.5),
        'ple_w_in': nrm((DEPTH, P_DIM, D_MODEL), P_DIM ** -0.5),
        'ple_norm': gain((DEPTH, D_MODEL)),
        'ple_w_gate': nrm((DEPTH, D_MODEL, D_MODEL), D_MODEL ** -0.5),
        'a_w_in': nrm((N_A, D_MODEL, PROJ_A), D_MODEL ** -0.5),
        'a_conv': nrm((N_A, CONV_W, QKV_DIM), CONV_W ** -0.5),
        'a_A_log': jnp.log(jax.random.uniform(next(keys), (N_A, H_A), f32, minval=1.0, maxval=16.0)),
        'a_dt_bias': dt + jnp.log(-jnp.expm1(-dt)),
        'a_o_norm': gain((N_A, DV)),
        'a_w_out': nrm((N_A, H_A * DV, D_MODEL), (H_A * DV) ** -0.5),
        'kv_norm': gain((D_MODEL,)),
        'w_kv': nrm((D_MODEL, 2 * H_B * HD_B), D_MODEL ** -0.5),
        'b_w_q': nrm((N_B, D_MODEL, H_B * HD_B), D_MODEL ** -0.5),
        'b_bias': SB_BIAS_INIT + 0.5 * jax.random.normal(next(keys), (N_B, H_B), f32),
        'b_w_o': nrm((N_B, H_B * HD_B, D_MODEL), (H_B * HD_B) ** -0.5),
        'final_norm': gain((D_MODEL,)),
    }


def reference(x_prompt, x_sample, cache_k, cache_v, state_conv, state_rec, page_table,
              p_prompt, p_sample, ffn1_norm, ffn1_w_in, ffn1_w_out, mix_norm, ffn2_norm,
              ffn2_w_in, ffn2_w_out, ple_w_in, ple_norm, ple_w_gate, a_w_in, a_conv, a_A_log,
              a_dt_bias, a_o_norm, a_w_out, kv_norm, w_kv, b_w_q, b_bias, b_w_o, final_norm):
    w = dict(ffn1_norm=ffn1_norm, ffn1_w_in=ffn1_w_in, ffn1_w_out=ffn1_w_out,
             mix_norm=mix_norm, ffn2_norm=ffn2_norm, ffn2_w_in=ffn2_w_in,
             ffn2_w_out=ffn2_w_out, ple_w_in=ple_w_in, ple_norm=ple_norm,
             ple_w_gate=ple_w_gate, a_w_in=a_w_in, a_conv=a_conv, a_A_log=a_A_log,
             a_dt_bias=a_dt_bias, a_o_norm=a_o_norm, a_w_out=a_w_out, kv_norm=kv_norm,
             w_kv=w_kv, b_w_q=b_w_q, b_bias=b_bias, b_w_o=b_w_o, final_norm=final_norm)

    bp = x_prompt.shape[0]
    conv0_p = [jnp.zeros((bp, CONV_W - 1, QKV_DIM), x_prompt.dtype) for _ in range(N_A)]
    rec0_p = [jnp.zeros((bp, H_A, DK, DV), x_prompt.dtype) for _ in range(N_A)]
    y_prompt, conv_prompt, rec_prompt, k_prompt, v_prompt = _trunk(
        x_prompt, p_prompt, conv0_p, rec0_p, None, None, w)

    db, n_pages = page_table.shape
    k_past = cache_k[page_table].reshape(db, n_pages * cache_k.shape[1], H_B, HD_B)
    v_past = cache_v[page_table].reshape(db, n_pages * cache_v.shape[1], H_B, HD_B)
    conv0_s = [state_conv[i] for i in range(N_A)]
    rec0_s = [state_rec[i] for i in range(N_A)]
    y_sample, conv_sample, rec_sample, k_sample, v_sample = _trunk(
        x_sample, p_sample, conv0_s, rec0_s, k_past, v_past, w)

    return (y_prompt, y_sample, conv_prompt, rec_prompt, k_prompt, v_prompt,
            conv_sample, rec_sample, k_sample, v_sample)
```

```python
import functools

import jax
import jax.numpy as jnp
from jax import lax
from jax.experimental import pallas as pl
from jax.experimental.pallas import tpu as pltpu

F32 = jnp.float32
BF16 = jnp.bfloat16
EPS = 1e-6

H_A = 8
DK = 128
DV = 128
QKV_DIM = H_A * (2 * DK + DV)
CONV_W = 4
CHUNK = 64
H_B = 16
HD_B = 64
PAGE = 128

LANES = 128
SUBLANES = 8
VMEM_LIMIT = 48 * 1024 * 1024

ROW_TILE = 1024
FF_TILE = 256
SB_BQ = 256
SB_BK = 256


def _cparams(sem):
    return pltpu.CompilerParams(dimension_semantics=sem, vmem_limit_bytes=VMEM_LIMIT)


def _rms(x, g):
    ms = jnp.mean(x * x, axis=-1, keepdims=True)
    return x * lax.rsqrt(ms + EPS) * g


def _sigmoid(x):
    return 1.0 / (1.0 + jnp.exp(-x))


def _softplus(x):
    return jnp.maximum(x, 0.0) + jnp.log1p(jnp.exp(-jnp.abs(x)))


def _dot(a, b):
    return jnp.dot(a.astype(BF16), b.astype(BF16), preferred_element_type=F32)


def _dot_nt(a, b):
    return lax.dot_general(a.astype(BF16), b.astype(BF16), (((1,), (1,)), ((), ())),
                           preferred_element_type=F32)


def _dot_tn(a, b):
    return lax.dot_general(a.astype(BF16), b.astype(BF16), (((0,), (0,)), ((), ())),
                           preferred_element_type=F32)


def _dot_f32(a, b):
    return jnp.dot(a, b, preferred_element_type=F32, precision=lax.Precision.HIGHEST)


def _ffn_body(x_ref, g_ref, wg_ref, wu_ref, wo_ref, o_ref, h_ref):
    @pl.when(pl.program_id(1) == 0)
    def _():
        x = x_ref[...]
        h_ref[...] = _rms(x, g_ref[...]).astype(BF16)
        o_ref[...] = x

    h = h_ref[...]
    gate = jnp.dot(h, wg_ref[...], preferred_element_type=F32)
    up = jnp.dot(h, wu_ref[...], preferred_element_type=F32)
    act = gate * _sigmoid(gate) * up
    o_ref[...] += 0.5 * _dot(act, wo_ref[...])


def _ffn(x, g, w_in, w_out):
    m, d = x.shape
    f = w_out.shape[0]
    nf = f // FF_TILE
    return pl.pallas_call(
        _ffn_body,
        grid=(m // ROW_TILE, nf),
        in_specs=[
            pl.BlockSpec((ROW_TILE, d), lambda i, j: (i, 0)),
            pl.BlockSpec((1, d), lambda i, j: (0, 0)),
            pl.BlockSpec((d, FF_TILE), lambda i, j: (0, j)),
            pl.BlockSpec((d, FF_TILE), lambda i, j: (0, j + nf)),
            pl.BlockSpec((FF_TILE, d), lambda i, j: (j, 0)),
        ],
        out_specs=pl.BlockSpec((ROW_TILE, d), lambda i, j: (i, 0)),
        out_shape=jax.ShapeDtypeStruct((m, d), F32),
        scratch_shapes=[pltpu.VMEM((ROW_TILE, d), BF16)],
        compiler_params=_cparams(("parallel", "arbitrary")),
        name="ffn",
    )(x, g.reshape(1, d), w_in, w_in, w_out)


def _nmm_body(x_ref, g_ref, w_ref, o_ref, h_ref):
    @pl.when(pl.program_id(1) == 0)
    def _():
        h_ref[...] = _rms(x_ref[...], g_ref[...]).astype(BF16)

    o_ref[...] = jnp.dot(h_ref[...], w_ref[...], preferred_element_type=F32)


def _norm_matmul(x, g, w, tn=512):
    m, d = x.shape
    n = w.shape[1]
    return pl.pallas_call(
        _nmm_body,
        grid=(m // ROW_TILE, n // tn),
        in_specs=[
            pl.BlockSpec((ROW_TILE, d), lambda i, j: (i, 0)),
            pl.BlockSpec((1, d), lambda i, j: (0, 0)),
            pl.BlockSpec((d, tn), lambda i, j: (0, j)),
        ],
        out_specs=pl.BlockSpec((ROW_TILE, tn), lambda i, j: (i, j)),
        out_shape=jax.ShapeDtypeStruct((m, n), F32),
        scratch_shapes=[pltpu.VMEM((ROW_TILE, d), BF16)],
        compiler_params=_cparams(("parallel", "arbitrary")),
        name="norm_matmul",
    )(x, g.reshape(1, d), w)


def _gdn_proj_body(x_ref, g_ref, w_ref, wab_ref, wabt_ref, prow_ref, pcol_ref,
                   o_ref, gates_ref, gatest_ref, h_ref):
    @pl.when(pl.program_id(1) == 0)
    def _():
        h = _rms(x_ref[...], g_ref[...]).astype(BF16)
        h_ref[...] = h
        ab = jnp.dot(h, wab_ref[...], preferred_element_type=F32)
        lane = lax.broadcasted_iota(jnp.int32, (1, LANES), 1)
        gv = -jnp.exp(prow_ref[0:1, :]) * _softplus(ab + prow_ref[1:2, :])
        gates_ref[...] = jnp.where(lane < H_A, gv, jnp.where(lane < 2 * H_A, _sigmoid(ab), 0.0))
        abt = lax.dot_general(wabt_ref[...], h, (((1,), (1,)), ((), ())),
                              preferred_element_type=F32)
        row = lax.broadcasted_iota(jnp.int32, (2 * H_A, 1), 0)
        gvt = -jnp.exp(pcol_ref[:, 0:1]) * _softplus(abt + pcol_ref[:, 1:2])
        gatest_ref[...] = jnp.where(row < H_A, gvt, _sigmoid(abt))

    o_ref[...] = jnp.dot(h_ref[...], w_ref[...], preferred_element_type=F32)


def _gdn_proj(x, g, w_main, w_ab, a_log, dt_bias, tn=512):
    m, d = x.shape
    n = w_main.shape[1]
    wab = jnp.pad(w_ab, ((0, 0), (0, LANES - 2 * H_A))).astype(BF16)
    wabt = w_ab.T.astype(BF16)
    zeros = jnp.zeros((H_A,), F32)
    prow = jnp.stack([jnp.pad(a_log, (0, LANES - H_A)), jnp.pad(dt_bias, (0, LANES - H_A))])
    prow = jnp.pad(prow, ((0, SUBLANES - 2), (0, 0)))
    pcol = jnp.stack([jnp.concatenate([a_log, zeros]), jnp.concatenate([dt_bias, zeros])], axis=1)
    pcol = jnp.pad(pcol, ((0, 0), (0, LANES - 2)))
    return pl.pallas_call(
        _gdn_proj_body,
        grid=(m // ROW_TILE, n // tn),
        in_specs=[
            pl.BlockSpec((ROW_TILE, d), lambda i, j: (i, 0)),
            pl.BlockSpec((1, d), lambda i, j: (0, 0)),
            pl.BlockSpec((d, tn), lambda i, j: (0, j)),
            pl.BlockSpec((d, LANES), lambda i, j: (0, 0)),
            pl.BlockSpec((2 * H_A, d), lambda i, j: (0, 0)),
            pl.BlockSpec((SUBLANES, LANES), lambda i, j: (0, 0)),
            pl.BlockSpec((2 * H_A, LANES), lambda i, j: (0, 0)),
        ],
        out_specs=[
            pl.BlockSpec((ROW_TILE, tn), lambda i, j: (i, j)),
            pl.BlockSpec((ROW_TILE, LANES), lambda i, j: (i, 0)),
            pl.BlockSpec((2 * H_A, ROW_TILE), lambda i, j: (0, i)),
        ],
        out_shape=[
            jax.ShapeDtypeStruct((m, n), F32),
            jax.ShapeDtypeStruct((m, LANES), F32),
            jax.ShapeDtypeStruct((2 * H_A, m), F32),
        ],
        scratch_shapes=[pltpu.VMEM((ROW_TILE, d), BF16)],
        compiler_params=_cparams(("parallel", "arbitrary")),
        name="gdn_proj",
    )(x, g.reshape(1, d), w_main, wab, wabt, prow, pcol)


def _mm_res_body(x_ref, y_ref, w_ref, o_ref):
    o_ref[...] = x_ref[...] + _dot(y_ref[...], w_ref[...])


def _matmul_residual(x, y, w):
    m, d = x.shape
    k = y.shape[1]
    return pl.pallas_call(
        _mm_res_body,
        grid=(m // ROW_TILE,),
        in_specs=[
            pl.BlockSpec((ROW_TILE, d), lambda i: (i, 0)),
            pl.BlockSpec((ROW_TILE, k), lambda i: (i, 0)),
            pl.BlockSpec((k, d), lambda i: (0, 0)),
        ],
        out_specs=pl.BlockSpec((ROW_TILE, d), lambda i: (i, 0)),
        out_shape=jax.ShapeDtypeStruct((m, d), F32),
        compiler_params=_cparams(("parallel",)),
        name="matmul_residual",
    )(x, y, w)


def _ple_body(x_ref, p_ref, g_ref, wpe_ref, wpg_ref, o_ref):
    x = x_ref[...]
    gate = _sigmoid(_dot(_rms(x, g_ref[...]), wpg_ref[...]))
    o_ref[...] = x + _dot(p_ref[...], wpe_ref[...]) * gate


def _ple_final_body(x_ref, p_ref, g_ref, wpe_ref, wpg_ref, gf_ref, o_ref):
    x = x_ref[...]
    gate = _sigmoid(_dot(_rms(x, g_ref[...]), wpg_ref[...]))
    o_ref[...] = _rms(x + _dot(p_ref[...], wpe_ref[...]) * gate, gf_ref[...])


def _ple(x, p, g, w_pe, w_pg, g_final=None):
    m, d = x.shape
    pd = p.shape[1]
    row = lambda i: (i, 0)
    fixed = lambda i: (0, 0)
    in_specs = [
        pl.BlockSpec((ROW_TILE, d), row),
        pl.BlockSpec((ROW_TILE, pd), row),
        pl.BlockSpec((1, d), fixed),
        pl.BlockSpec((pd, d), fixed),
        pl.BlockSpec((d, d), fixed),
    ]
    args = [x, p, g.reshape(1, d), w_pe, w_pg]
    body = _ple_body
    if g_final is not None:
        in_specs.append(pl.BlockSpec((1, d), fixed))
        args.append(g_final.reshape(1, d))
        body = _ple_final_body
    return pl.pallas_call(
        body,
        grid=(m // ROW_TILE,),
        in_specs=in_specs,
        out_specs=pl.BlockSpec((ROW_TILE, d), row),
        out_shape=jax.ShapeDtypeStruct((m, d), F32),
        compiler_params=_cparams(("parallel",)),
        name="ple",
    )(*args)


def _div(x, n):
    assert n & (n - 1) == 0
    return lax.shift_right_logical(x, n.bit_length() - 1)


def _chunk_masks(seq_len):
    i = lax.broadcasted_iota(jnp.int32, (CHUNK, CHUNK), 0)
    j = lax.broadcasted_iota(jnp.int32, (CHUNK, CHUNK), 1)
    same = _div(i, seq_len) == _div(j, seq_len)
    incl = same & (i >= j)
    strict = same & (i > j)
    return i, j, same, incl, strict


def _unit_lower_inverse(a, i, j, seq_len):
    base = SUBLANES
    eye = (i == j).astype(F32)
    dg = jnp.where(_div(i, base) == _div(j, base), a, 0.0)
    p1 = _dot_f32(dg, dg)
    p2 = _dot_f32(p1, p1)
    y = eye - dg
    y = y + _dot_f32(y, p1)
    x = y + _dot_f32(y, p2)
    b = base
    while b < seq_len:
        cm = jnp.where((_div(i, 2 * b) == _div(j, 2 * b)) & (_div(i, b) != _div(j, b)), a, 0.0)
        x = x - _dot_f32(_dot_f32(x, cm), x)
        b *= 2
    return x


def _gdn_head_pre(c, gates, gc_cols, gc_rows, gl_cols, h, masks, seq_len):
    i, j, same, incl, strict = masks
    qh = c[:, h * DK:(h + 1) * DK]
    kh = c[:, H_A * DK + h * DK:H_A * DK + (h + 1) * DK]
    vh = c[:, 2 * H_A * DK + h * DV:2 * H_A * DK + (h + 1) * DV]
    qh = qh * lax.rsqrt(jnp.sum(qh * qh, axis=-1, keepdims=True) + EPS) * (DK ** -0.5)
    kh = kh * lax.rsqrt(jnp.sum(kh * kh, axis=-1, keepdims=True) + EPS)
    gcc = gc_cols[:, h:h + 1]
    gcr = gc_rows[h:h + 1, :]
    gl = gl_cols[:, h:h + 1]
    beta = gates[:, H_A + h:H_A + h + 1]
    decay = jnp.where(incl, jnp.exp(jnp.where(incl, gcc - gcr, 0.0)), 0.0)
    kb = kh * beta
    a = jnp.where(strict, _dot_nt(kb, kh) * decay, 0.0)
    qk = _dot_nt(qh, kh) * decay
    t = _unit_lower_inverse(a, i, j, seq_len)
    eg = jnp.exp(gcc)
    sol = _dot_f32(t, jnp.concatenate([vh * beta, kb * eg], axis=1))
    u = sol[:, :DV]
    w = sol[:, DV:]
    q_dec = qh * eg
    k_dec = kh * jnp.exp(gl - gcc)
    return u, w, qk, q_dec, k_dec, gl


def _gdn_out(o, onorm, gate):
    on = o * lax.rsqrt(jnp.mean(o * o, axis=-1, keepdims=True) + EPS) * onorm
    return on * (gate * _sigmoid(gate))


def _gdn_prompt_body(proj_ref, gates_ref, gt_ref, cw_ref, on_ref, o_ref, s_out_ref, s_ref, tail_ref):
    n = pl.program_id(1)

    @pl.when(n == 0)
    def _():
        s_ref[...] = jnp.zeros_like(s_ref)
        tail_ref[...] = jnp.zeros_like(tail_ref)

    x = proj_ref[:, :QKV_DIM]
    xs = jnp.concatenate([tail_ref[...], x], axis=0)
    cw = cw_ref[...]
    conv = x * cw[CONV_W - 1:CONV_W, :]
    for s in range(1, CONV_W):
        conv = conv + pltpu.roll(xs, s, axis=0)[SUBLANES:, :] * cw[CONV_W - 1 - s:CONV_W - s, :]
    tail_ref[...] = x[CHUNK - SUBLANES:, :]
    c = conv * _sigmoid(conv)

    masks = _chunk_masks(CHUNK)
    i, j, same, incl, strict = masks
    gates = gates_ref[...]
    gc_cols = _dot_f32(incl.astype(F32), gates)
    gc_rows = _dot_f32(gt_ref[0], (i <= j).astype(F32))
    gl_cols = jnp.broadcast_to(gc_cols[CHUNK - 1:CHUNK, :], gc_cols.shape)
    onorm = on_ref[...]
    for h in range(H_A):
        u, w, qk, q_dec, k_dec, gl = _gdn_head_pre(c, gates, gc_cols, gc_rows, gl_cols, h,
                                                    masks, CHUNK)
        sh = s_ref[h]
        v_new = u - _dot(w, sh)
        o = _dot(q_dec, sh) + _dot(qk, v_new)
        s_ref[h] = sh * jnp.exp(gl[0:1, :]) + _dot_tn(k_dec, v_new)
        gate = proj_ref[:, QKV_DIM + h * DV:QKV_DIM + (h + 1) * DV]
        o_ref[:, h * DV:(h + 1) * DV] = _gdn_out(o, onorm, gate)

    @pl.when(n == pl.num_programs(1) - 1)
    def _():
        s_out_ref[0] = s_ref[...]


def _gdn_prompt(proj, gates, gates_t, conv_w, o_norm, bsz, t_len):
    nc = t_len // CHUNK
    n_proj = proj.shape[1]
    row = lambda b, n: (b * nc + n, 0)
    return pl.pallas_call(
        _gdn_prompt_body,
        grid=(bsz, nc),
        in_specs=[
            pl.BlockSpec((CHUNK, n_proj), row),
            pl.BlockSpec((CHUNK, LANES), row),
            pl.BlockSpec((1, 2 * H_A, CHUNK), lambda b, n: (b * nc + n, 0, 0)),
            pl.BlockSpec((CONV_W, QKV_DIM), lambda b, n: (0, 0)),
            pl.BlockSpec((1, DV), lambda b, n: (0, 0)),
        ],
        out_specs=[
            pl.BlockSpec((CHUNK, H_A * DV), row),
            pl.BlockSpec((1, H_A, DK, DV), lambda b, n: (b, 0, 0, 0)),
        ],
        out_shape=[
            jax.ShapeDtypeStruct((bsz * t_len, H_A * DV), F32),
            jax.ShapeDtypeStruct((bsz, H_A, DK, DV), F32),
        ],
        scratch_shapes=[pltpu.VMEM((H_A, DK, DV), F32), pltpu.VMEM((SUBLANES, QKV_DIM), F32)],
        compiler_params=_cparams(("parallel", "arbitrary")),
        name="gdn_prompt",
    )(proj, gates, gates_t, conv_w, o_norm.reshape(1, DV))


def _gdn_sample_body(proj_ref, gates_ref, gt_ref, cw_ref, on_ref, prev_ref, s_in_ref,
                     o_ref, s_out_ref, *, seq_len):
    nseq = CHUNK // seq_len
    x = proj_ref[:, :QKV_DIM]
    prev = prev_ref[...].reshape(CHUNK, QKV_DIM)
    cw = cw_ref[...]
    r = lax.broadcasted_iota(jnp.int32, (CHUNK, 1), 0) & (seq_len - 1)
    conv = x * cw[CONV_W - 1:CONV_W, :]
    for s in range(1, CONV_W):
        shifted = jnp.where(r < s, pltpu.roll(prev, (s - seq_len) % CHUNK, axis=0),
                            pltpu.roll(x, s, axis=0))
        conv = conv + shifted * cw[CONV_W - 1 - s:CONV_W - s, :]
    c = conv * _sigmoid(conv)

    masks = _chunk_masks(seq_len)
    i, j, same, incl, strict = masks
    gates = gates_ref[...]
    gc_cols = _dot_f32(incl.astype(F32), gates)
    gc_rows = _dot_f32(gt_ref[0], (same & (i <= j)).astype(F32))
    gl_cols = _dot_f32(same.astype(F32), gates)
    onorm = on_ref[...]
    rows = _div(lax.broadcasted_iota(jnp.int32, (CHUNK, 1), 0), seq_len)
    for h in range(H_A):
        u, w, qk, q_dec, k_dec, gl = _gdn_head_pre(c, gates, gc_cols, gc_rows, gl_cols, h,
                                                    masks, seq_len)
        lhs = jnp.concatenate([w, q_dec], axis=0)
        ws = jnp.zeros((CHUNK, DV), F32)
        qs = jnp.zeros((CHUNK, DV), F32)
        for q in range(nseq):
            both = _dot(lhs, s_in_ref[q, h])
            mine = rows == q
            ws = jnp.where(mine, both[:CHUNK], ws)
            qs = jnp.where(mine, both[CHUNK:], qs)
        v_new = u - ws
        o = qs + _dot(qk, v_new)
        for q in range(nseq):
            mine = rows == q
            g_tot = jnp.exp(gl[q * seq_len:q * seq_len + 1, :])
            s_out_ref[q, h] = s_in_ref[q, h] * g_tot + _dot_tn(jnp.where(mine, k_dec, 0.0), v_new)
        gate = proj_ref[:, QKV_DIM + h * DV:QKV_DIM + (h + 1) * DV]
        o_ref[:, h * DV:(h + 1) * DV] = _gdn_out(o, onorm, gate)


def _gdn_sample(proj, gates, gates_t, conv_w, o_norm, prev8, s0, row0, bsz, t_len):
    nseq = CHUNK // t_len
    steps = bsz // nseq
    blk0 = row0 // CHUNK
    n_proj = proj.shape[1]
    row = lambda g: (blk0 + g, 0)
    return pl.pallas_call(
        functools.partial(_gdn_sample_body, seq_len=t_len),
        grid=(steps,),
        in_specs=[
            pl.BlockSpec((CHUNK, n_proj), row),
            pl.BlockSpec((CHUNK, LANES), row),
            pl.BlockSpec((1, 2 * H_A, CHUNK), lambda g: (blk0 + g, 0, 0)),
            pl.BlockSpec((CONV_W, QKV_DIM), lambda g: (0, 0)),
            pl.BlockSpec((1, DV), lambda g: (0, 0)),
            pl.BlockSpec((nseq, SUBLANES, QKV_DIM), lambda g: (g, 0, 0)),
            pl.BlockSpec((nseq, H_A, DK, DV), lambda g: (g, 0, 0, 0)),
        ],
        out_specs=[
            pl.BlockSpec((CHUNK, H_A * DV), lambda g: (g, 0)),
            pl.BlockSpec((nseq, H_A, DK, DV), lambda g: (g, 0, 0, 0)),
        ],
        out_shape=[
            jax.ShapeDtypeStruct((bsz * t_len, H_A * DV), F32),
            jax.ShapeDtypeStruct((bsz, H_A, DK, DV), F32),
        ],
        compiler_params=_cparams(("parallel",)),
        name="gdn_sample",
    )(proj, gates, gates_t, conv_w, o_norm.reshape(1, DV), prev8, s0)


def _sb_block(z, mask, run, u, v):
    sp = _softplus(z)
    if mask is not None:
        sp = jnp.where(mask, sp, 0.0)
    hi = sp.astype(BF16)
    lo = (sp - hi.astype(F32)).astype(BF16)
    csum = run + jnp.dot(hi, u, preferred_element_type=F32) + jnp.dot(lo, u, preferred_element_type=F32)
    att = jnp.exp(z - csum)
    if mask is not None:
        att = jnp.where(mask, att, 0.0)
    return _dot(att, v), csum[:, 0:1]


def _sb_prompt_body(qt_ref, kt_ref, bias_ref, q_ref, k_ref, v_ref, u_ref, o_ref, acc_ref, run_ref):
    hp = pl.program_id(1)
    p = pl.program_id(2)
    qi = qt_ref[p]
    kj = kt_ref[p]
    heads = LANES // HD_B

    @pl.when(kj == qi)
    def _():
        acc_ref[...] = jnp.zeros_like(acc_ref)
        run_ref[...] = jnp.zeros_like(run_ref)

    q = q_ref[...] * (HD_B ** -0.5)
    k = k_ref[...].astype(BF16)
    v = v_ref[...]
    u = u_ref[...]
    lane = _div(lax.broadcasted_iota(jnp.int32, (1, LANES), 1), HD_B)

    def step(masked):
        mask = None
        if masked:
            qpos = lax.broadcasted_iota(jnp.int32, (SB_BQ, SB_BK), 0)
            kpos = lax.broadcasted_iota(jnp.int32, (SB_BQ, SB_BK), 1)
            mask = kpos < qpos
        acc = acc_ref[...]
        for a in range(heads):
            mine = lane == a
            z = _dot_nt(jnp.where(mine, q, 0.0), k) + bias_ref[hp * heads + a]
            pv, run = _sb_block(z, mask, run_ref[a], u, jnp.where(mine, v, 0.0))
            acc = acc + pv
            run_ref[a] = run
        acc_ref[...] = acc

    @pl.when(kj == qi)
    def _():
        step(True)

    @pl.when(kj != qi)
    def _():
        step(False)

    @pl.when(kj == 0)
    def _():
        o_ref[...] = acc_ref[...]


def _sb_prompt(q, k, v, bias, bsz, t_len):
    d = H_B * HD_B
    nq = t_len // SB_BQ
    assert SB_BQ == SB_BK
    pairs = [(qi, kj) for qi in range(nq) for kj in range(qi, -1, -1)]
    qt = jnp.array([pq for pq, _ in pairs], jnp.int32)
    kt = jnp.array([pk for _, pk in pairs], jnp.int32)
    ii = lax.broadcasted_iota(jnp.int32, (SB_BK, SB_BK), 0)
    jj = lax.broadcasted_iota(jnp.int32, (SB_BK, SB_BK), 1)
    u = (ii >= jj).astype(BF16)
    heads = LANES // HD_B
    grid_spec = pltpu.PrefetchScalarGridSpec(
        num_scalar_prefetch=2,
        grid=(bsz, d // LANES, len(pairs)),
        in_specs=[
            pl.BlockSpec(memory_space=pltpu.SMEM),
            pl.BlockSpec((SB_BQ, LANES), lambda b, h, p, qt, kt: (b * nq + qt[p], h)),
            pl.BlockSpec((SB_BK, LANES), lambda b, h, p, qt, kt: (b * nq + kt[p], h)),
            pl.BlockSpec((SB_BK, LANES), lambda b, h, p, qt, kt: (b * nq + kt[p], h)),
            pl.BlockSpec((SB_BK, SB_BK), lambda b, h, p, qt, kt: (0, 0)),
        ],
        out_specs=pl.BlockSpec((SB_BQ, LANES), lambda b, h, p, qt, kt: (b * nq + qt[p], h)),
        scratch_shapes=[pltpu.VMEM((SB_BQ, LANES), F32), pltpu.VMEM((heads, SB_BQ, 1), F32)],
    )
    return pl.pallas_call(
        _sb_prompt_body,
        grid_spec=grid_spec,
        out_shape=jax.ShapeDtypeStruct((bsz * t_len, d), F32),
        compiler_params=_cparams(("parallel", "parallel", "arbitrary")),
        name="sb_prompt",
    )(qt, kt, bias, q, k, v, u)


def _sb_sample_body(pt_ref, bias_ref, q_ref, kn_ref, vn_ref, kp_ref, vp_ref, u_ref, o_ref,
                    qbd_ref, acc_ref, run_ref, bcol_ref, *, t_len):
    s = pl.program_id(1)
    d = H_B * HD_B
    rows = H_B * t_len
    rowh = _div(lax.broadcasted_iota(jnp.int32, (rows, 1), 0), t_len)
    colh = _div(lax.broadcasted_iota(jnp.int32, (1, d), 1), HD_B)
    u = u_ref[...]

    def block(kb, vb, mask):
        z = _dot_nt(qbd_ref[...], kb) + bcol_ref[...]
        pv, run = _sb_block(z, mask, run_ref[...], u, vb)
        acc_ref[...] += pv
        run_ref[...] = run

    @pl.when(s == 0)
    def _():
        q = q_ref[0] * (HD_B ** -0.5)
        qbd_ref[...] = jnp.where(rowh == colh, jnp.concatenate([q] * H_B, axis=0), 0.0).astype(BF16)
        bcol = jnp.zeros((rows, 1), F32)
        for h in range(H_B):
            bcol = jnp.where(rowh == h, bias_ref[h], bcol)
        bcol_ref[...] = bcol
        acc_ref[...] = jnp.zeros_like(acc_ref)
        run_ref[...] = jnp.zeros_like(run_ref)
        pad = jnp.zeros((PAGE - t_len, d), F32)
        t = lax.broadcasted_iota(jnp.int32, (rows, PAGE), 0) & (t_len - 1)
        jk = lax.broadcasted_iota(jnp.int32, (rows, PAGE), 1)
        block(jnp.concatenate([kn_ref[0], pad], axis=0), jnp.concatenate([vn_ref[0], pad], axis=0),
              jk < t)

    block(kp_ref[0], vp_ref[0], None)

    @pl.when(s == pl.num_programs(1) - 1)
    def _():
        g = jnp.where(rowh == colh, acc_ref[...], 0.0)
        o_ref[0] = jnp.sum(g.reshape(H_B, t_len, d), axis=0)


def _sb_sample(q, k_new, v_new, cache_k, cache_v, page_table, bias, bsz, t_len):
    d = H_B * HD_B
    n_pages = page_table.shape[1]
    rows = H_B * t_len
    ii = lax.broadcasted_iota(jnp.int32, (PAGE, PAGE), 0)
    jj = lax.broadcasted_iota(jnp.int32, (PAGE, PAGE), 1)
    u = (ii >= jj).astype(BF16)
    tok = lambda b, s, pt: (b, 0, 0)
    page = lambda b, s, pt: (pt[b * n_pages + (n_pages - 1 - s)], 0, 0)
    grid_spec = pltpu.PrefetchScalarGridSpec(
        num_scalar_prefetch=1,
        grid=(bsz, n_pages),
        in_specs=[
            pl.BlockSpec(memory_space=pltpu.SMEM),
            pl.BlockSpec((1, t_len, d), tok),
            pl.BlockSpec((1, t_len, d), tok),
            pl.BlockSpec((1, t_len, d), tok),
            pl.BlockSpec((1, PAGE, d), page),
            pl.BlockSpec((1, PAGE, d), page),
            pl.BlockSpec((PAGE, PAGE), lambda b, s, pt: (0, 0)),
        ],
        out_specs=pl.BlockSpec((1, t_len, d), tok),
        scratch_shapes=[
            pltpu.VMEM((rows, d), BF16),
            pltpu.VMEM((rows, d), F32),
            pltpu.VMEM((rows, 1), F32),
            pltpu.VMEM((rows, 1), F32),
        ],
    )
    out = pl.pallas_call(
        functools.partial(_sb_sample_body, t_len=t_len),
        grid_spec=grid_spec,
        out_shape=jax.ShapeDtypeStruct((bsz, t_len, d), F32),
        compiler_params=_cparams(("parallel", "arbitrary")),
        name="sb_sample",
    )(page_table.reshape(-1), bias, q.reshape(bsz, t_len, d), k_new.reshape(bsz, t_len, d),
      v_new.reshape(bsz, t_len, d), cache_k, cache_v, u)
    return out.reshape(bsz * t_len, d)


def kernel(x_prompt, x_sample, cache_k, cache_v, state_conv, state_rec, page_table, p_prompt, p_sample, ffn1_norm, ffn1_w_in, ffn1_w_out, mix_norm, ffn2_norm, ffn2_w_in, ffn2_w_out, ple_w_in, ple_norm, ple_w_gate, a_w_in, a_conv, a_A_log, a_dt_bias, a_o_norm, a_w_out, kv_norm, w_kv, b_w_q, b_bias, b_w_o, final_norm):
    bp, tp, d = x_prompt.shape
    bs, ts, _ = x_sample.shape
    depth = ffn1_norm.shape[0]
    n_a = a_w_in.shape[0]
    mp = bp * tp
    ms = bs * ts
    dkv = H_B * HD_B
    n_main = QKV_DIM + H_A * DV

    x = jnp.concatenate([x_prompt.reshape(mp, d), x_sample.reshape(ms, d)], axis=0)
    p_all = jnp.concatenate([p_prompt.reshape(depth, mp, -1), p_sample.reshape(depth, ms, -1)], axis=1)
    ck = cache_k.reshape(cache_k.shape[0], cache_k.shape[1], dkv)
    cv = cache_v.reshape(cache_v.shape[0], cache_v.shape[1], dkv)
    prev8 = jnp.pad(state_conv, ((0, 0), (0, 0), (SUBLANES - (CONV_W - 1), 0), (0, 0)))

    convs_p, convs_s, recs_p, recs_s = [], [], [], []
    k_all = v_all = None
    for i in range(depth):
        x = _ffn(x, ffn1_norm[i], ffn1_w_in[i].astype(BF16), ffn1_w_out[i].astype(BF16))
        if i < n_a:
            proj, gates, gates_t = _gdn_proj(
                x, mix_norm[i], a_w_in[i][:, :n_main].astype(BF16), a_w_in[i][:, n_main:],
                a_A_log[i], a_dt_bias[i])
            gt = gates_t.reshape(2 * H_A, -1, CHUNK).transpose(1, 0, 2)
            o_p, s_p = _gdn_prompt(proj, gates, gt, a_conv[i], a_o_norm[i], bp, tp)
            o_s, s_s = _gdn_sample(proj, gates, gt, a_conv[i], a_o_norm[i], prev8[i], state_rec[i],
                                   mp, bs, ts)
            convs_p.append(proj[:mp, :QKV_DIM].reshape(bp, tp, QKV_DIM)[:, tp - (CONV_W - 1):])
            convs_s.append(proj[mp:, :QKV_DIM].reshape(bs, ts, QKV_DIM)[:, ts - (CONV_W - 1):])
            recs_p.append(s_p)
            recs_s.append(s_s)
            x = _matmul_residual(x, jnp.concatenate([o_p, o_s], axis=0), a_w_out[i].astype(BF16))
        else:
            ib = i - n_a
            q = _norm_matmul(x, mix_norm[i], b_w_q[ib].astype(BF16))
            o_p = _sb_prompt(q, k_all, v_all, b_bias[ib], bp, tp)
            o_s = _sb_sample(q[mp:], k_all[mp:], v_all[mp:], ck, cv, page_table, b_bias[ib], bs, ts)
            x = _matmul_residual(x, jnp.concatenate([o_p, o_s], axis=0), b_w_o[ib].astype(BF16))
        x = _ffn(x, ffn2_norm[i], ffn2_w_in[i].astype(BF16), ffn2_w_out[i].astype(BF16))
        x = _ple(x, p_all[i], ple_norm[i], ple_w_in[i].astype(BF16), ple_w_gate[i].astype(BF16),
                 final_norm if i == depth - 1 else None)
        if i == n_a - 1:
            kv = _norm_matmul(x, kv_norm, w_kv.astype(BF16))
            k_all = kv[:, :dkv]
            v_all = kv[:, dkv:]

    y_prompt = x[:mp].reshape(bp, tp, d)
    y_sample = x[mp:].reshape(bs, ts, d)
    k_prompt = k_all[:mp].reshape(bp, tp, H_B, HD_B)
    v_prompt = v_all[:mp].reshape(bp, tp, H_B, HD_B)
    k_sample = k_all[mp:].reshape(bs, ts, H_B, HD_B)
    v_sample = v_all[mp:].reshape(bs, ts, H_B, HD_B)
    return (y_prompt, y_sample, jnp.stack(convs_p), jnp.stack(recs_p), k_prompt, v_prompt,
            jnp.stack(convs_s), jnp.stack(recs_s), k_sample, v_sample)
```

```python
import functools

import jax
import jax.numpy as jnp
from jax import lax
from jax.experimental import pallas as pl
from jax.experimental.pallas import tpu as pltpu

F32 = jnp.float32
BF16 = jnp.bfloat16
EPS = 1e-6

H_A = 8
DK = 128
DV = 128
QKV_DIM = H_A * (2 * DK + DV)
CONV_W = 4
CHUNK = 64
H_B = 16
HD_B = 64
PAGE = 128

LANES = 128
SUBLANES = 8
VMEM_LIMIT = 48 * 1024 * 1024

ROW_TILE = 1024
FF_TILE = 256
SB_BQ = 1024
SB_BK = 256
SB_PAGES = 4


def _cparams(sem):
    return pltpu.CompilerParams(dimension_semantics=sem, vmem_limit_bytes=VMEM_LIMIT)


def _rms(x, g):
    ms = jnp.mean(x * x, axis=-1, keepdims=True)
    return x * lax.rsqrt(ms + EPS) * g


def _sigmoid(x):
    return 1.0 / (1.0 + jnp.exp(-x))


def _softplus(x):
    return jnp.maximum(x, 0.0) + jnp.log1p(jnp.exp(-jnp.abs(x)))


def _dot(a, b):
    return jnp.dot(a.astype(BF16), b.astype(BF16), preferred_element_type=F32)


def _dot_nt(a, b):
    return lax.dot_general(a.astype(BF16), b.astype(BF16), (((1,), (1,)), ((), ())),
                           preferred_element_type=F32)


def _dot_tn(a, b):
    return lax.dot_general(a.astype(BF16), b.astype(BF16), (((0,), (0,)), ((), ())),
                           preferred_element_type=F32)


def _dot_f32(a, b):
    return jnp.dot(a, b, preferred_element_type=F32, precision=lax.Precision.HIGHEST)


def _ffn_body(x_ref, g_ref, wg_ref, wu_ref, wo_ref, o_ref, h_ref):
    @pl.when(pl.program_id(1) == 0)
    def _():
        x = x_ref[...]
        h_ref[...] = _rms(x, g_ref[...]).astype(BF16)
        o_ref[...] = x

    h = h_ref[...]
    gate = jnp.dot(h, wg_ref[...], preferred_element_type=F32)
    up = jnp.dot(h, wu_ref[...], preferred_element_type=F32)
    act = gate * _sigmoid(gate) * up
    o_ref[...] += 0.5 * _dot(act, wo_ref[...])


def _ffn(x, g, w_in, w_out):
    m, d = x.shape
    f = w_out.shape[0]
    nf = f // FF_TILE
    return pl.pallas_call(
        _ffn_body,
        grid=(m // ROW_TILE, nf),
        in_specs=[
            pl.BlockSpec((ROW_TILE, d), lambda i, j: (i, 0)),
            pl.BlockSpec((1, d), lambda i, j: (0, 0)),
            pl.BlockSpec((d, FF_TILE), lambda i, j: (0, j)),
            pl.BlockSpec((d, FF_TILE), lambda i, j: (0, j + nf)),
            pl.BlockSpec((FF_TILE, d), lambda i, j: (j, 0)),
        ],
        out_specs=pl.BlockSpec((ROW_TILE, d), lambda i, j: (i, 0)),
        out_shape=jax.ShapeDtypeStruct((m, d), F32),
        scratch_shapes=[pltpu.VMEM((ROW_TILE, d), BF16)],
        compiler_params=_cparams(("parallel", "arbitrary")),
        name="ffn",
    )(x, g.reshape(1, d), w_in, w_in, w_out)


def _nmm_body(x_ref, g_ref, w_ref, o_ref, h_ref):
    @pl.when(pl.program_id(1) == 0)
    def _():
        h_ref[...] = _rms(x_ref[...], g_ref[...]).astype(BF16)

    o_ref[...] = jnp.dot(h_ref[...], w_ref[...], preferred_element_type=F32)


def _norm_matmul(x, g, w, tn=512):
    m, d = x.shape
    n = w.shape[1]
    return pl.pallas_call(
        _nmm_body,
        grid=(m // ROW_TILE, n // tn),
        in_specs=[
            pl.BlockSpec((ROW_TILE, d), lambda i, j: (i, 0)),
            pl.BlockSpec((1, d), lambda i, j: (0, 0)),
            pl.BlockSpec((d, tn), lambda i, j: (0, j)),
        ],
        out_specs=pl.BlockSpec((ROW_TILE, tn), lambda i, j: (i, j)),
        out_shape=jax.ShapeDtypeStruct((m, n), F32),
        scratch_shapes=[pltpu.VMEM((ROW_TILE, d), BF16)],
        compiler_params=_cparams(("parallel", "arbitrary")),
        name="norm_matmul",
    )(x, g.reshape(1, d), w)


def _gdn_proj_body(x_ref, g_ref, w_ref, wab_ref, wabt_ref, prow_ref, pcol_ref,
                   o_ref, gates_ref, gatest_ref, h_ref):
    @pl.when(pl.program_id(1) == 0)
    def _():
        h = _rms(x_ref[...], g_ref[...]).astype(BF16)
        h_ref[...] = h
        ab = jnp.dot(h, wab_ref[...], preferred_element_type=F32)
        lane = lax.broadcasted_iota(jnp.int32, (1, LANES), 1)
        gv = -jnp.exp(prow_ref[0:1, :]) * _softplus(ab + prow_ref[1:2, :])
        gates_ref[...] = jnp.where(lane < H_A, gv, jnp.where(lane < 2 * H_A, _sigmoid(ab), 0.0))
        abt = lax.dot_general(wabt_ref[...], h, (((1,), (1,)), ((), ())),
                              preferred_element_type=F32)
        row = lax.broadcasted_iota(jnp.int32, (2 * H_A, 1), 0)
        gvt = -jnp.exp(pcol_ref[:, 0:1]) * _softplus(abt + pcol_ref[:, 1:2])
        gatest_ref[...] = jnp.where(row < H_A, gvt, _sigmoid(abt))

    o_ref[...] = jnp.dot(h_ref[...], w_ref[...], preferred_element_type=F32)


def _gdn_proj(x, g, w_main, w_ab, a_log, dt_bias, tn=512):
    m, d = x.shape
    n = w_main.shape[1]
    wab = jnp.pad(w_ab, ((0, 0), (0, LANES - 2 * H_A))).astype(BF16)
    wabt = w_ab.T.astype(BF16)
    zeros = jnp.zeros((H_A,), F32)
    prow = jnp.stack([jnp.pad(a_log, (0, LANES - H_A)), jnp.pad(dt_bias, (0, LANES - H_A))])
    prow = jnp.pad(prow, ((0, SUBLANES - 2), (0, 0)))
    pcol = jnp.stack([jnp.concatenate([a_log, zeros]), jnp.concatenate([dt_bias, zeros])], axis=1)
    pcol = jnp.pad(pcol, ((0, 0), (0, LANES - 2)))
    return pl.pallas_call(
        _gdn_proj_body,
        grid=(m // ROW_TILE, n // tn),
        in_specs=[
            pl.BlockSpec((ROW_TILE, d), lambda i, j: (i, 0)),
            pl.BlockSpec((1, d), lambda i, j: (0, 0)),
            pl.BlockSpec((d, tn), lambda i, j: (0, j)),
            pl.BlockSpec((d, LANES), lambda i, j: (0, 0)),
            pl.BlockSpec((2 * H_A, d), lambda i, j: (0, 0)),
            pl.BlockSpec((SUBLANES, LANES), lambda i, j: (0, 0)),
            pl.BlockSpec((2 * H_A, LANES), lambda i, j: (0, 0)),
        ],
        out_specs=[
            pl.BlockSpec((ROW_TILE, tn), lambda i, j: (i, j)),
            pl.BlockSpec((ROW_TILE, LANES), lambda i, j: (i, 0)),
            pl.BlockSpec((2 * H_A, ROW_TILE), lambda i, j: (0, i)),
        ],
        out_shape=[
            jax.ShapeDtypeStruct((m, n), F32),
            jax.ShapeDtypeStruct((m, LANES), F32),
            jax.ShapeDtypeStruct((2 * H_A, m), F32),
        ],
        scratch_shapes=[pltpu.VMEM((ROW_TILE, d), BF16)],
        compiler_params=_cparams(("parallel", "arbitrary")),
        name="gdn_proj",
    )(x, g.reshape(1, d), w_main, wab, wabt, prow, pcol)


def _mm_res_body(x_ref, y_ref, w_ref, o_ref):
    o_ref[...] = x_ref[...] + _dot(y_ref[...], w_ref[...])


def _matmul_residual(x, y, w):
    m, d = x.shape
    k = y.shape[1]
    return pl.pallas_call(
        _mm_res_body,
        grid=(m // ROW_TILE,),
        in_specs=[
            pl.BlockSpec((ROW_TILE, d), lambda i: (i, 0)),
            pl.BlockSpec((ROW_TILE, k), lambda i: (i, 0)),
            pl.BlockSpec((k, d), lambda i: (0, 0)),
        ],
        out_specs=pl.BlockSpec((ROW_TILE, d), lambda i: (i, 0)),
        out_shape=jax.ShapeDtypeStruct((m, d), F32),
        compiler_params=_cparams(("parallel",)),
        name="matmul_residual",
    )(x, y, w)


def _ple_body(x_ref, p_ref, g_ref, wpe_ref, wpg_ref, o_ref):
    x = x_ref[...]
    gate = _sigmoid(_dot(_rms(x, g_ref[...]), wpg_ref[...]))
    o_ref[...] = x + _dot(p_ref[...], wpe_ref[...]) * gate


def _ple_final_body(x_ref, p_ref, g_ref, wpe_ref, wpg_ref, gf_ref, o_ref):
    x = x_ref[...]
    gate = _sigmoid(_dot(_rms(x, g_ref[...]), wpg_ref[...]))
    o_ref[...] = _rms(x + _dot(p_ref[...], wpe_ref[...]) * gate, gf_ref[...])


def _ple(x, p, g, w_pe, w_pg, g_final=None):
    m, d = x.shape
    pd = p.shape[1]
    row = lambda i: (i, 0)
    fixed = lambda i: (0, 0)
    in_specs = [
        pl.BlockSpec((ROW_TILE, d), row),
        pl.BlockSpec((ROW_TILE, pd), row),
        pl.BlockSpec((1, d), fixed),
        pl.BlockSpec((pd, d), fixed),
        pl.BlockSpec((d, d), fixed),
    ]
    args = [x, p, g.reshape(1, d), w_pe, w_pg]
    body = _ple_body
    if g_final is not None:
        in_specs.append(pl.BlockSpec((1, d), fixed))
        args.append(g_final.reshape(1, d))
        body = _ple_final_body
    return pl.pallas_call(
        body,
        grid=(m // ROW_TILE,),
        in_specs=in_specs,
        out_specs=pl.BlockSpec((ROW_TILE, d), row),
        out_shape=jax.ShapeDtypeStruct((m, d), F32),
        compiler_params=_cparams(("parallel",)),
        name="ple",
    )(*args)


def _div(x, n):
    assert n & (n - 1) == 0
    return lax.shift_right_logical(x, n.bit_length() - 1)


def _chunk_masks(seq_len):
    i = lax.broadcasted_iota(jnp.int32, (CHUNK, CHUNK), 0)
    j = lax.broadcasted_iota(jnp.int32, (CHUNK, CHUNK), 1)
    same = _div(i, seq_len) == _div(j, seq_len)
    incl = same & (i >= j)
    strict = same & (i > j)
    return i, j, same, incl, strict


def _unit_lower_inverse(a, i, j, seq_len):
    base = SUBLANES
    eye = (i == j).astype(F32)
    dg = jnp.where(_div(i, base) == _div(j, base), a, 0.0)
    p1 = _bdot(dg, dg)
    p2 = _bdot(p1, p1)
    y = eye - dg
    y = y + _bdot(y, p1)
    x = y + _bdot(y, p2)
    b = base
    while b < seq_len:
        cm = jnp.where((_div(i, 2 * b) == _div(j, 2 * b)) & (_div(i, b) != _div(j, b)), a, 0.0)
        x = x - _bdot(_bdot(x, cm), x)
        b *= 2
    return x


def _bdot(a, b):
    return lax.dot_general(a.astype(BF16), b.astype(BF16), (((2,), (1,)), ((0,), (0,))),
                           preferred_element_type=F32)


def _bdot_nt(a, b):
    return lax.dot_general(a.astype(BF16), b.astype(BF16), (((2,), (2,)), ((0,), (0,))),
                           preferred_element_type=F32)


def _bdot_tn(a, b):
    return lax.dot_general(a.astype(BF16), b.astype(BF16), (((1,), (1,)), ((0,), (0,))),
                           preferred_element_type=F32)


def _split(a):
    hi = a.astype(BF16)
    return hi, (a - hi.astype(F32)).astype(BF16)


def _bdot3(a, b):
    ah, al = _split(a)
    bh, bl = _split(b)
    return _bdot(ah, bh) + (_bdot(ah, bl) + _bdot(al, bh))


def _heads(x, width, first=0):
    return jnp.stack([x[:, first + h * width:first + (h + 1) * width] for h in range(H_A)])


def _gdn_pre(c, gates, gc_cols, gc_rows, gl_cols, masks, seq_len):
    i, j, same, incl, strict = masks
    q = _heads(c, DK)
    k = _heads(c, DK, H_A * DK)
    v = _heads(c, DV, 2 * H_A * DK)
    q = q * lax.rsqrt(jnp.sum(q * q, axis=-1, keepdims=True) + EPS) * (DK ** -0.5)
    k = k * lax.rsqrt(jnp.sum(k * k, axis=-1, keepdims=True) + EPS)
    gcc = _heads(gc_cols, 1)
    gl = _heads(gl_cols, 1)
    beta = _heads(gates, 1, H_A)
    gcr = jnp.stack([gc_rows[h:h + 1, :] for h in range(H_A)])
    decay = jnp.where(incl, jnp.exp(jnp.where(incl, gcc - gcr, 0.0)), 0.0)
    kb = k * beta
    a = jnp.where(strict, _bdot_nt(kb, k) * decay, 0.0)
    qk = _bdot_nt(q, k) * decay
    x = _unit_lower_inverse(a, i, j, seq_len)
    eg = jnp.exp(gcc)
    rhs = jnp.concatenate([v * beta, kb * eg], axis=2)
    sol0 = _bdot(x, rhs)
    sol = sol0 + _bdot(x, rhs - sol0 - _bdot3(a, sol0))
    u = sol[:, :, :DV]
    w = sol[:, :, DV:]
    q_dec = q * eg
    k_dec = k * jnp.exp(gl - gcc)
    return u, w, qk, q_dec, k_dec, gl


def _gdn_store(o_ref, o, onorm, gate):
    on = o * lax.rsqrt(jnp.mean(o * o, axis=-1, keepdims=True) + EPS) * onorm
    out = on * (gate * _sigmoid(gate))
    for h in range(H_A):
        o_ref[:, h * DV:(h + 1) * DV] = out[h]


def _gdn_prompt_body(proj_ref, gates_ref, gt_ref, cw_ref, on_ref, o_ref, s_out_ref, s_ref, tail_ref):
    n = pl.program_id(1)

    @pl.when(n == 0)
    def _():
        s_ref[...] = jnp.zeros_like(s_ref)
        tail_ref[0:SUBLANES, :] = jnp.zeros((SUBLANES, QKV_DIM), F32)

    x = proj_ref[:, :QKV_DIM]
    tail_ref[SUBLANES:, :] = x
    cw = cw_ref[...]
    conv = x * cw[CONV_W - 1:CONV_W, :]
    for s in range(1, CONV_W):
        conv = conv + tail_ref[SUBLANES - s:SUBLANES - s + CHUNK, :] * cw[CONV_W - 1 - s:CONV_W - s, :]
    tail_ref[0:SUBLANES, :] = x[CHUNK - SUBLANES:, :]
    c = conv * _sigmoid(conv)

    masks = _chunk_masks(CHUNK)
    i, j, same, incl, strict = masks
    gates = gates_ref[...]
    gc_cols = _dot_f32(incl.astype(F32), gates)
    gc_rows = _dot_f32(gt_ref[0], (i <= j).astype(F32))
    gl_cols = jnp.broadcast_to(gc_cols[CHUNK - 1:CHUNK, :], gc_cols.shape)
    u, w, qk, q_dec, k_dec, gl = _gdn_pre(c, gates, gc_cols, gc_rows, gl_cols, masks, CHUNK)
    s = s_ref[...]
    both = _bdot(jnp.concatenate([w, q_dec], axis=1), s)
    v_new = u - both[:, :CHUNK]
    o = both[:, CHUNK:] + _bdot(qk, v_new)
    s_ref[...] = s * jnp.exp(gl[:, 0:1, :]) + _bdot_tn(k_dec, v_new)
    _gdn_store(o_ref, o, on_ref[...], _heads(proj_ref[:, QKV_DIM:], DV))

    @pl.when(n == pl.num_programs(1) - 1)
    def _():
        s_out_ref[0] = s_ref[...]


def _gdn_prompt(proj, gates, gates_t, conv_w, o_norm, bsz, t_len):
    nc = t_len // CHUNK
    n_proj = proj.shape[1]
    row = lambda b, n: (b * nc + n, 0)
    return pl.pallas_call(
        _gdn_prompt_body,
        grid=(bsz, nc),
        in_specs=[
            pl.BlockSpec((CHUNK, n_proj), row),
            pl.BlockSpec((CHUNK, LANES), row),
            pl.BlockSpec((1, 2 * H_A, CHUNK), lambda b, n: (b * nc + n, 0, 0)),
            pl.BlockSpec((CONV_W, QKV_DIM), lambda b, n: (0, 0)),
            pl.BlockSpec((1, DV), lambda b, n: (0, 0)),
        ],
        out_specs=[
            pl.BlockSpec((CHUNK, H_A * DV), row),
            pl.BlockSpec((1, H_A, DK, DV), lambda b, n: (b, 0, 0, 0)),
        ],
        out_shape=[
            jax.ShapeDtypeStruct((bsz * t_len, H_A * DV), F32),
            jax.ShapeDtypeStruct((bsz, H_A, DK, DV), F32),
        ],
        scratch_shapes=[pltpu.VMEM((H_A, DK, DV), F32), pltpu.VMEM((SUBLANES + CHUNK, QKV_DIM), F32)],
        compiler_params=_cparams(("parallel", "arbitrary")),
        name="gdn_prompt",
    )(proj, gates, gates_t, conv_w, o_norm.reshape(1, DV))


def _gdn_sample_body(proj_ref, gates_ref, gt_ref, cw_ref, on_ref, prev_ref, s_in_ref,
                     o_ref, s_out_ref, *, seq_len):
    nseq = CHUNK // seq_len
    x = proj_ref[:, :QKV_DIM]
    prev = prev_ref[...].reshape(CHUNK, QKV_DIM)
    cw = cw_ref[...]
    r = lax.broadcasted_iota(jnp.int32, (CHUNK, 1), 0) & (seq_len - 1)
    conv = x * cw[CONV_W - 1:CONV_W, :]
    for s in range(1, CONV_W):
        shifted = jnp.where(r < s, pltpu.roll(prev, (s - seq_len) % CHUNK, axis=0),
                            pltpu.roll(x, s, axis=0))
        conv = conv + shifted * cw[CONV_W - 1 - s:CONV_W - s, :]
    c = conv * _sigmoid(conv)

    masks = _chunk_masks(seq_len)
    i, j, same, incl, strict = masks
    gates = gates_ref[...]
    gc_cols = _dot_f32(incl.astype(F32), gates)
    gc_rows = _dot_f32(gt_ref[0], (same & (i <= j)).astype(F32))
    gl_cols = _dot_f32(same.astype(F32), gates)
    rows = _div(lax.broadcasted_iota(jnp.int32, (2 * CHUNK, 1), 0) & (CHUNK - 1), seq_len)
    u, w, qk, q_dec, k_dec, gl = _gdn_pre(c, gates, gc_cols, gc_rows, gl_cols, masks, seq_len)
    lhs = jnp.concatenate([w, q_dec], axis=1)
    both = jnp.zeros((H_A, 2 * CHUNK, DV), F32)
    for q in range(nseq):
        both = jnp.where(rows == q, _bdot(lhs, s_in_ref[q]), both)
    v_new = u - both[:, :CHUNK]
    o = both[:, CHUNK:] + _bdot(qk, v_new)
    for q in range(nseq):
        g_tot = jnp.exp(gl[:, q * seq_len:q * seq_len + 1, :])
        s_out_ref[q] = s_in_ref[q] * g_tot + _bdot_tn(jnp.where(rows[:CHUNK] == q, k_dec, 0.0), v_new)
    _gdn_store(o_ref, o, on_ref[...], _heads(proj_ref[:, QKV_DIM:], DV))


def _gdn_sample(proj, gates, gates_t, conv_w, o_norm, prev8, s0, row0, bsz, t_len):
    nseq = CHUNK // t_len
    steps = bsz // nseq
    blk0 = row0 // CHUNK
    n_proj = proj.shape[1]
    row = lambda g: (blk0 + g, 0)
    return pl.pallas_call(
        functools.partial(_gdn_sample_body, seq_len=t_len),
        grid=(steps,),
        in_specs=[
            pl.BlockSpec((CHUNK, n_proj), row),
            pl.BlockSpec((CHUNK, LANES), row),
            pl.BlockSpec((1, 2 * H_A, CHUNK), lambda g: (blk0 + g, 0, 0)),
            pl.BlockSpec((CONV_W, QKV_DIM), lambda g: (0, 0)),
            pl.BlockSpec((1, DV), lambda g: (0, 0)),
            pl.BlockSpec((nseq, SUBLANES, QKV_DIM), lambda g: (g, 0, 0)),
            pl.BlockSpec((nseq, H_A, DK, DV), lambda g: (g, 0, 0, 0)),
        ],
        out_specs=[
            pl.BlockSpec((CHUNK, H_A * DV), lambda g: (g, 0)),
            pl.BlockSpec((nseq, H_A, DK, DV), lambda g: (g, 0, 0, 0)),
        ],
        out_shape=[
            jax.ShapeDtypeStruct((bsz * t_len, H_A * DV), F32),
            jax.ShapeDtypeStruct((bsz, H_A, DK, DV), F32),
        ],
        compiler_params=_cparams(("parallel",)),
        name="gdn_sample",
    )(proj, gates, gates_t, conv_w, o_norm.reshape(1, DV), prev8, s0)


LOG2E = 1.4426950408889634
SB_QSCALE = (HD_B ** -0.5) * LOG2E


def _softplus2(x):
    neg_abs = lax.bitcast_convert_type(
        lax.bitcast_convert_type(x, jnp.uint32) | jnp.uint32(0x80000000), F32)
    return jnp.maximum(x, 0.0) + jnp.log2(1.0 + jnp.exp2(neg_abs))


def _sb_mass(z, mask, u):
    sp = _softplus2(z)
    if mask is not None:
        sp = jnp.where(mask, sp, 0.0)
    return jnp.dot(sp.astype(BF16), u, preferred_element_type=F32)


def _sb_weights(z, mask, csum):
    att = jnp.exp2(z - csum)
    if mask is not None:
        att = jnp.where(mask, att, 0.0)
    return att


def _sb_prompt_body(qt_ref, kt_ref, bias_ref, q_ref, k_ref, v_ref, u_ref, o_ref, acc_ref, run_ref):
    hp = pl.program_id(1)
    p = pl.program_id(2)
    qi = qt_ref[p]
    kj = kt_ref[p]
    heads = LANES // HD_B
    ratio = SB_BQ // SB_BK
    diag = kj >= qi * ratio

    @pl.when(kj == qi * ratio + (ratio - 1))
    def _():
        acc_ref[...] = jnp.zeros_like(acc_ref)
        run_ref[...] = jnp.zeros_like(run_ref)

    q = q_ref[...] * SB_QSCALE
    k = k_ref[...].astype(BF16)
    v = v_ref[...]
    u = u_ref[...]
    lane = _div(lax.broadcasted_iota(jnp.int32, (1, LANES), 1), HD_B)

    def step(masked):
        mask = None
        if masked:
            qpos = lax.broadcasted_iota(jnp.int32, (SB_BQ, SB_BK), 0) + qi * SB_BQ
            kpos = lax.broadcasted_iota(jnp.int32, (SB_BQ, SB_BK), 1) + kj * SB_BK
            mask = kpos < qpos
        acc = acc_ref[...]
        for a in range(heads):
            mine = lane == a
            z = _dot_nt(jnp.where(mine, q, 0.0), k) + bias_ref[hp * heads + a] * LOG2E
            csum = run_ref[a] + _sb_mass(z, mask, u)
            acc = acc + _dot(_sb_weights(z, mask, csum), jnp.where(mine, v, 0.0))
            run_ref[a] = csum[:, 0:1]
        acc_ref[...] = acc

    @pl.when(diag)
    def _():
        step(True)

    @pl.when(jnp.logical_not(diag))
    def _():
        step(False)

    @pl.when(kj == 0)
    def _():
        o_ref[...] = acc_ref[...]


def _sb_prompt(q, k, v, bias, bsz, t_len):
    d = H_B * HD_B
    nq = t_len // SB_BQ
    nk = t_len // SB_BK
    ratio = SB_BQ // SB_BK
    pairs = [(qi, kj) for qi in range(nq) for kj in range(qi * ratio + ratio - 1, -1, -1)]
    qt = jnp.array([pq for pq, _ in pairs], jnp.int32)
    kt = jnp.array([pk for _, pk in pairs], jnp.int32)
    ii = lax.broadcasted_iota(jnp.int32, (SB_BK, SB_BK), 0)
    jj = lax.broadcasted_iota(jnp.int32, (SB_BK, SB_BK), 1)
    u = (ii >= jj).astype(BF16)
    heads = LANES // HD_B
    grid_spec = pltpu.PrefetchScalarGridSpec(
        num_scalar_prefetch=2,
        grid=(bsz, d // LANES, len(pairs)),
        in_specs=[
            pl.BlockSpec(memory_space=pltpu.SMEM),
            pl.BlockSpec((SB_BQ, LANES), lambda b, h, p, qt, kt: (b * nq + qt[p], h)),
            pl.BlockSpec((SB_BK, LANES), lambda b, h, p, qt, kt: (b * nk + kt[p], h)),
            pl.BlockSpec((SB_BK, LANES), lambda b, h, p, qt, kt: (b * nk + kt[p], h)),
            pl.BlockSpec((SB_BK, SB_BK), lambda b, h, p, qt, kt: (0, 0)),
        ],
        out_specs=pl.BlockSpec((SB_BQ, LANES), lambda b, h, p, qt, kt: (b * nq + qt[p], h)),
        scratch_shapes=[pltpu.VMEM((SB_BQ, LANES), F32), pltpu.VMEM((heads, SB_BQ, 1), F32)],
    )
    return pl.pallas_call(
        _sb_prompt_body,
        grid_spec=grid_spec,
        out_shape=jax.ShapeDtypeStruct((bsz * t_len, d), F32),
        compiler_params=_cparams(("parallel", "parallel", "arbitrary")),
        name="sb_prompt",
    )(qt, kt, bias, q, k, v, u)


def _sb_sample_body(pt_ref, bias_ref, q_ref, kn_ref, vn_ref, *rest, t_len):
    kp_refs = rest[:SB_PAGES]
    vp_refs = rest[SB_PAGES:2 * SB_PAGES]
    u_ref, o_ref, qbd_ref, acc_ref, run_ref, bcol_ref = rest[2 * SB_PAGES:]
    s = pl.program_id(1)
    d = H_B * HD_B
    rows = H_B * t_len
    rowh = _div(lax.broadcasted_iota(jnp.int32, (rows, 1), 0), t_len)
    colh = _div(lax.broadcasted_iota(jnp.int32, (1, d), 1), HD_B)
    u = u_ref[...]

    def blocks(kbs, vbs, masks):
        zs = [_dot_nt(qbd_ref[...], kb) + bcol_ref[...] for kb in kbs]
        masses = [_sb_mass(z, m, u) for z, m in zip(zs, masks)]
        run = run_ref[...]
        pv = acc_ref[...]
        for z, m, mass, vb in zip(zs, masks, masses, vbs):
            csum = run + mass
            pv = pv + _dot(_sb_weights(z, m, csum), vb)
            run = csum[:, 0:1]
        acc_ref[...] = pv
        run_ref[...] = run

    pages_k = [r[0] for r in kp_refs]
    pages_v = [r[0] for r in vp_refs]

    @pl.when(s == 0)
    def _():
        q = q_ref[0] * SB_QSCALE
        qbd_ref[...] = jnp.where(rowh == colh, jnp.concatenate([q] * H_B, axis=0), 0.0).astype(BF16)
        bcol = jnp.zeros((rows, 1), F32)
        for h in range(H_B):
            bcol = jnp.where(rowh == h, bias_ref[h] * LOG2E, bcol)
        bcol_ref[...] = bcol
        acc_ref[...] = jnp.zeros_like(acc_ref)
        run_ref[...] = jnp.zeros_like(run_ref)
        pad = jnp.zeros((PAGE - t_len, d), F32)
        t = lax.broadcasted_iota(jnp.int32, (rows, PAGE), 0) & (t_len - 1)
        jk = lax.broadcasted_iota(jnp.int32, (rows, PAGE), 1)
        blocks([jnp.concatenate([kn_ref[0], pad], axis=0)] + pages_k,
               [jnp.concatenate([vn_ref[0], pad], axis=0)] + pages_v,
               [jk < t] + [None] * SB_PAGES)

    @pl.when(s != 0)
    def _():
        blocks(pages_k, pages_v, [None] * SB_PAGES)

    @pl.when(s == pl.num_programs(1) - 1)
    def _():
        g = jnp.where(rowh == colh, acc_ref[...], 0.0)
        o_ref[0] = jnp.sum(g.reshape(H_B, t_len, d), axis=0)


def _sb_sample(q, k_new, v_new, cache_k, cache_v, page_table, bias, bsz, t_len):
    d = H_B * HD_B
    n_pages = page_table.shape[1]
    rows = H_B * t_len
    ii = lax.broadcasted_iota(jnp.int32, (PAGE, PAGE), 0)
    jj = lax.broadcasted_iota(jnp.int32, (PAGE, PAGE), 1)
    u = (ii >= jj).astype(BF16)
    tok = lambda b, s, pt: (b, 0, 0)

    def page(p):
        return lambda b, s, pt: (pt[b * n_pages + (n_pages - 1 - (s * SB_PAGES + p))], 0, 0)

    page_specs = [pl.BlockSpec((1, PAGE, d), page(p)) for p in range(SB_PAGES)]
    grid_spec = pltpu.PrefetchScalarGridSpec(
        num_scalar_prefetch=1,
        grid=(bsz, n_pages // SB_PAGES),
        in_specs=[
            pl.BlockSpec(memory_space=pltpu.SMEM),
            pl.BlockSpec((1, t_len, d), tok),
            pl.BlockSpec((1, t_len, d), tok),
            pl.BlockSpec((1, t_len, d), tok),
            *page_specs,
            *page_specs,
            pl.BlockSpec((PAGE, PAGE), lambda b, s, pt: (0, 0)),
        ],
        out_specs=pl.BlockSpec((1, t_len, d), tok),
        scratch_shapes=[
            pltpu.VMEM((rows, d), BF16),
            pltpu.VMEM((rows, d), F32),
            pltpu.VMEM((rows, 1), F32),
            pltpu.VMEM((rows, 1), F32),
        ],
    )
    out = pl.pallas_call(
        functools.partial(_sb_sample_body, t_len=t_len),
        grid_spec=grid_spec,
        out_shape=jax.ShapeDtypeStruct((bsz, t_len, d), F32),
        compiler_params=_cparams(("parallel", "arbitrary")),
        name="sb_sample",
    )(page_table.reshape(-1), bias, q.reshape(bsz, t_len, d), k_new.reshape(bsz, t_len, d),
      v_new.reshape(bsz, t_len, d), *([cache_k] * SB_PAGES), *([cache_v] * SB_PAGES), u)
    return out.reshape(bsz * t_len, d)


def kernel(x_prompt, x_sample, cache_k, cache_v, state_conv, state_rec, page_table, p_prompt, p_sample, ffn1_norm, ffn1_w_in, ffn1_w_out, mix_norm, ffn2_norm, ffn2_w_in, ffn2_w_out, ple_w_in, ple_norm, ple_w_gate, a_w_in, a_conv, a_A_log, a_dt_bias, a_o_norm, a_w_out, kv_norm, w_kv, b_w_q, b_bias, b_w_o, final_norm):
    bp, tp, d = x_prompt.shape
    bs, ts, _ = x_sample.shape
    depth = ffn1_norm.shape[0]
    n_a = a_w_in.shape[0]
    mp = bp * tp
    ms = bs * ts
    dkv = H_B * HD_B
    n_main = QKV_DIM + H_A * DV

    x = jnp.concatenate([x_prompt.reshape(mp, d), x_sample.reshape(ms, d)], axis=0)
    p_all = jnp.concatenate([p_prompt.reshape(depth, mp, -1), p_sample.reshape(depth, ms, -1)], axis=1)
    ck = cache_k.reshape(cache_k.shape[0], cache_k.shape[1], dkv).astype(BF16)
    cv = cache_v.reshape(cache_v.shape[0], cache_v.shape[1], dkv).astype(BF16)
    prev8 = jnp.pad(state_conv, ((0, 0), (0, 0), (SUBLANES - (CONV_W - 1), 0), (0, 0)))

    convs_p, convs_s, recs_p, recs_s = [], [], [], []
    k_all = v_all = None
    for i in range(depth):
        x = _ffn(x, ffn1_norm[i], ffn1_w_in[i].astype(BF16), ffn1_w_out[i].astype(BF16))
        if i < n_a:
            proj, gates, gates_t = _gdn_proj(
                x, mix_norm[i], a_w_in[i][:, :n_main].astype(BF16), a_w_in[i][:, n_main:],
                a_A_log[i], a_dt_bias[i])
            gt = gates_t.reshape(2 * H_A, -1, CHUNK).transpose(1, 0, 2)
            o_p, s_p = _gdn_prompt(proj, gates, gt, a_conv[i], a_o_norm[i], bp, tp)
            o_s, s_s = _gdn_sample(proj, gates, gt, a_conv[i], a_o_norm[i], prev8[i], state_rec[i],
                                   mp, bs, ts)
            convs_p.append(proj[:mp, :QKV_DIM].reshape(bp, tp, QKV_DIM)[:, tp - (CONV_W - 1):])
            convs_s.append(proj[mp:, :QKV_DIM].reshape(bs, ts, QKV_DIM)[:, ts - (CONV_W - 1):])
            recs_p.append(s_p)
            recs_s.append(s_s)
            x = _matmul_residual(x, jnp.concatenate([o_p, o_s], axis=0), a_w_out[i].astype(BF16))
        else:
            ib = i - n_a
            q = _norm_matmul(x, mix_norm[i], b_w_q[ib].astype(BF16))
            o_p = _sb_prompt(q, k_all, v_all, b_bias[ib], bp, tp)
            o_s = _sb_sample(q[mp:], k_all[mp:], v_all[mp:], ck, cv, page_table, b_bias[ib], bs, ts)
            x = _matmul_residual(x, jnp.concatenate([o_p, o_s], axis=0), b_w_o[ib].astype(BF16))
        x = _ffn(x, ffn2_norm[i], ffn2_w_in[i].astype(BF16), ffn2_w_out[i].astype(BF16))
        x = _ple(x, p_all[i], ple_norm[i], ple_w_in[i].astype(BF16), ple_w_gate[i].astype(BF16),
                 final_norm if i == depth - 1 else None)
        if i == n_a - 1:
            k_all = _norm_matmul(x, kv_norm, w_kv[:, :dkv].astype(BF16))
            v_all = _norm_matmul(x, kv_norm, w_kv[:, dkv:].astype(BF16))

    y_prompt = x[:mp].reshape(bp, tp, d)
    y_sample = x[mp:].reshape(bs, ts, d)
    k_prompt = k_all[:mp].reshape(bp, tp, H_B, HD_B)
    v_prompt = v_all[:mp].reshape(bp, tp, H_B, HD_B)
    k_sample = k_all[mp:].reshape(bs, ts, H_B, HD_B)
    v_sample = v_all[mp:].reshape(bs, ts, H_B, HD_B)
    return (y_prompt, y_sample, jnp.stack(convs_p), jnp.stack(recs_p), k_prompt, v_prompt,
            jnp.stack(convs_s), jnp.stack(recs_s), k_sample, v_sample)
```

```python
import functools

import jax
import jax.numpy as jnp
from jax import lax
from jax.experimental import pallas as pl
from jax.experimental.pallas import tpu as pltpu

F32 = jnp.float32
BF16 = jnp.bfloat16
EPS = 1e-6

H_A = 8
DK = 128
DV = 128
QKV_DIM = H_A * (2 * DK + DV)
CONV_W = 4
CHUNK = 64
H_B = 16
HD_B = 64
PAGE = 128

LANES = 128
SUBLANES = 8
VMEM_LIMIT = 48 * 1024 * 1024

ROW_TILE = 1024
FF_TILE = 256
SB_BQ = 1024
SB_BK = 256
SB_PAGES = 4


def _cparams(sem):
    return pltpu.CompilerParams(dimension_semantics=sem, vmem_limit_bytes=VMEM_LIMIT)


def _rms(x, g):
    ms = jnp.mean(x * x, axis=-1, keepdims=True)
    return x * lax.rsqrt(ms + EPS) * g


def _sigmoid(x):
    return 1.0 / (1.0 + jnp.exp(-x))


def _softplus(x):
    return jnp.maximum(x, 0.0) + jnp.log1p(jnp.exp(-jnp.abs(x)))


def _dot(a, b):
    return jnp.dot(a.astype(BF16), b.astype(BF16), preferred_element_type=F32)


def _dot_nt(a, b):
    return lax.dot_general(a.astype(BF16), b.astype(BF16), (((1,), (1,)), ((), ())),
                           preferred_element_type=F32)


def _dot_tn(a, b):
    return lax.dot_general(a.astype(BF16), b.astype(BF16), (((0,), (0,)), ((), ())),
                           preferred_element_type=F32)


def _dot_f32(a, b):
    return jnp.dot(a, b, preferred_element_type=F32, precision=lax.Precision.HIGHEST)


def _ffn_body(x_ref, g_ref, wg_ref, wu_ref, wo_ref, o_ref, h_ref):
    @pl.when(pl.program_id(1) == 0)
    def _():
        x = x_ref[...]
        h_ref[...] = _rms(x, g_ref[...]).astype(BF16)
        o_ref[...] = x

    h = h_ref[...]
    gate = jnp.dot(h, wg_ref[...], preferred_element_type=F32)
    up = jnp.dot(h, wu_ref[...], preferred_element_type=F32)
    act = gate * _sigmoid(gate) * up
    o_ref[...] += 0.5 * _dot(act, wo_ref[...])


def _ffn(x, g, w_in, w_out):
    m, d = x.shape
    f = w_out.shape[0]
    nf = f // FF_TILE
    return pl.pallas_call(
        _ffn_body,
        grid=(m // ROW_TILE, nf),
        in_specs=[
            pl.BlockSpec((ROW_TILE, d), lambda i, j: (i, 0)),
            pl.BlockSpec((1, d), lambda i, j: (0, 0)),
            pl.BlockSpec((d, FF_TILE), lambda i, j: (0, j)),
            pl.BlockSpec((d, FF_TILE), lambda i, j: (0, j + nf)),
            pl.BlockSpec((FF_TILE, d), lambda i, j: (j, 0)),
        ],
        out_specs=pl.BlockSpec((ROW_TILE, d), lambda i, j: (i, 0)),
        out_shape=jax.ShapeDtypeStruct((m, d), F32),
        scratch_shapes=[pltpu.VMEM((ROW_TILE, d), BF16)],
        compiler_params=_cparams(("parallel", "arbitrary")),
        name="ffn",
    )(x, g.reshape(1, d), w_in, w_in, w_out)


def _nmm_body(x_ref, g_ref, w_ref, o_ref, h_ref):
    @pl.when(pl.program_id(1) == 0)
    def _():
        h_ref[...] = _rms(x_ref[...], g_ref[...]).astype(BF16)

    o_ref[...] = jnp.dot(h_ref[...], w_ref[...], preferred_element_type=F32)


def _norm_matmul(x, g, w, tn=512):
    m, d = x.shape
    n = w.shape[1]
    return pl.pallas_call(
        _nmm_body,
        grid=(m // ROW_TILE, n // tn),
        in_specs=[
            pl.BlockSpec((ROW_TILE, d), lambda i, j: (i, 0)),
            pl.BlockSpec((1, d), lambda i, j: (0, 0)),
            pl.BlockSpec((d, tn), lambda i, j: (0, j)),
        ],
        out_specs=pl.BlockSpec((ROW_TILE, tn), lambda i, j: (i, j)),
        out_shape=jax.ShapeDtypeStruct((m, n), F32),
        scratch_shapes=[pltpu.VMEM((ROW_TILE, d), BF16)],
        compiler_params=_cparams(("parallel", "arbitrary")),
        name="norm_matmul",
    )(x, g.reshape(1, d), w)


def _gdn_proj_body(x_ref, g_ref, w_ref, wab_ref, wabt_ref, prow_ref, pcol_ref,
                   o_ref, gates_ref, gatest_ref, h_ref):
    @pl.when(pl.program_id(1) == 0)
    def _():
        h = _rms(x_ref[...], g_ref[...]).astype(BF16)
        h_ref[...] = h
        ab = jnp.dot(h, wab_ref[...], preferred_element_type=F32)
        lane = lax.broadcasted_iota(jnp.int32, (1, LANES), 1)
        gv = -jnp.exp(prow_ref[0:1, :]) * _softplus(ab + prow_ref[1:2, :])
        gates_ref[...] = jnp.where(lane < H_A, gv, jnp.where(lane < 2 * H_A, _sigmoid(ab), 0.0))
        abt = lax.dot_general(wabt_ref[...], h, (((1,), (1,)), ((), ())),
                              preferred_element_type=F32)
        row = lax.broadcasted_iota(jnp.int32, (2 * H_A, 1), 0)
        gvt = -jnp.exp(pcol_ref[:, 0:1]) * _softplus(abt + pcol_ref[:, 1:2])
        gatest_ref[...] = jnp.where(row < H_A, gvt, _sigmoid(abt))

    o_ref[...] = jnp.dot(h_ref[...], w_ref[...], preferred_element_type=F32)


def _gdn_proj(x, g, w_main, w_ab, a_log, dt_bias, tn=512):
    m, d = x.shape
    n = w_main.shape[1]
    wab = jnp.pad(w_ab, ((0, 0), (0, LANES - 2 * H_A))).astype(BF16)
    wabt = w_ab.T.astype(BF16)
    zeros = jnp.zeros((H_A,), F32)
    prow = jnp.stack([jnp.pad(a_log, (0, LANES - H_A)), jnp.pad(dt_bias, (0, LANES - H_A))])
    prow = jnp.pad(prow, ((0, SUBLANES - 2), (0, 0)))
    pcol = jnp.stack([jnp.concatenate([a_log, zeros]), jnp.concatenate([dt_bias, zeros])], axis=1)
    pcol = jnp.pad(pcol, ((0, 0), (0, LANES - 2)))
    return pl.pallas_call(
        _gdn_proj_body,
        grid=(m // ROW_TILE, n // tn),
        in_specs=[
            pl.BlockSpec((ROW_TILE, d), lambda i, j: (i, 0)),
            pl.BlockSpec((1, d), lambda i, j: (0, 0)),
            pl.BlockSpec((d, tn), lambda i, j: (0, j)),
            pl.BlockSpec((d, LANES), lambda i, j: (0, 0)),
            pl.BlockSpec((2 * H_A, d), lambda i, j: (0, 0)),
            pl.BlockSpec((SUBLANES, LANES), lambda i, j: (0, 0)),
            pl.BlockSpec((2 * H_A, LANES), lambda i, j: (0, 0)),
        ],
        out_specs=[
            pl.BlockSpec((ROW_TILE, tn), lambda i, j: (i, j)),
            pl.BlockSpec((ROW_TILE, LANES), lambda i, j: (i, 0)),
            pl.BlockSpec((2 * H_A, ROW_TILE), lambda i, j: (0, i)),
        ],
        out_shape=[
            jax.ShapeDtypeStruct((m, n), F32),
            jax.ShapeDtypeStruct((m, LANES), F32),
            jax.ShapeDtypeStruct((2 * H_A, m), F32),
        ],
        scratch_shapes=[pltpu.VMEM((ROW_TILE, d), BF16)],
        compiler_params=_cparams(("parallel", "arbitrary")),
        name="gdn_proj",
    )(x, g.reshape(1, d), w_main, wab, wabt, prow, pcol)


def _mm_res_body(x_ref, ya_ref, yb_ref, w_ref, o_ref, *, tiles_a):
    i = pl.program_id(0)

    @pl.when(i < tiles_a)
    def _():
        o_ref[...] = x_ref[...] + _dot(ya_ref[...], w_ref[...])

    @pl.when(i >= tiles_a)
    def _():
        o_ref[...] = x_ref[...] + _dot(yb_ref[...], w_ref[...])


def _matmul_residual(x, ya, yb, w):
    m, d = x.shape
    k = ya.shape[1]
    tiles_a = ya.shape[0] // ROW_TILE
    assert ya.shape[0] % ROW_TILE == 0 and yb.shape[0] % ROW_TILE == 0
    return pl.pallas_call(
        functools.partial(_mm_res_body, tiles_a=tiles_a),
        grid=(m // ROW_TILE,),
        in_specs=[
            pl.BlockSpec((ROW_TILE, d), lambda i: (i, 0)),
            pl.BlockSpec((ROW_TILE, k), lambda i: (jnp.minimum(i, tiles_a - 1), 0)),
            pl.BlockSpec((ROW_TILE, k), lambda i: (jnp.maximum(i - tiles_a, 0), 0)),
            pl.BlockSpec((k, d), lambda i: (0, 0)),
        ],
        out_specs=pl.BlockSpec((ROW_TILE, d), lambda i: (i, 0)),
        out_shape=jax.ShapeDtypeStruct((m, d), F32),
        compiler_params=_cparams(("arbitrary",)),
        name="matmul_residual",
    )(x, ya, yb, w)


def _ple_body(x_ref, p_ref, g_ref, wpe_ref, wpg_ref, o_ref):
    x = x_ref[...]
    gate = _sigmoid(_dot(_rms(x, g_ref[...]), wpg_ref[...]))
    o_ref[...] = x + _dot(p_ref[...], wpe_ref[...]) * gate


def _ple_final_body(x_ref, p_ref, g_ref, wpe_ref, wpg_ref, gf_ref, oa_ref, ob_ref, *, tiles_a):
    x = x_ref[...]
    gate = _sigmoid(_dot(_rms(x, g_ref[...]), wpg_ref[...]))
    y = _rms(x + _dot(p_ref[...], wpe_ref[...]) * gate, gf_ref[...])
    i = pl.program_id(0)

    @pl.when(i < tiles_a)
    def _():
        oa_ref[...] = y

    @pl.when(i >= tiles_a)
    def _():
        ob_ref[...] = y


def _ple(x, p, g, w_pe, w_pg):
    m, d = x.shape
    pd = p.shape[1]
    row = lambda i: (i, 0)
    fixed = lambda i: (0, 0)
    return pl.pallas_call(
        _ple_body,
        grid=(m // ROW_TILE,),
        in_specs=[
            pl.BlockSpec((ROW_TILE, d), row),
            pl.BlockSpec((ROW_TILE, pd), row),
            pl.BlockSpec((1, d), fixed),
            pl.BlockSpec((pd, d), fixed),
            pl.BlockSpec((d, d), fixed),
        ],
        out_specs=pl.BlockSpec((ROW_TILE, d), row),
        out_shape=jax.ShapeDtypeStruct((m, d), F32),
        compiler_params=_cparams(("parallel",)),
        name="ple",
    )(x, p, g.reshape(1, d), w_pe, w_pg)


def _ple_final(x, p, g, w_pe, w_pg, g_final, rows_a):
    m, d = x.shape
    pd = p.shape[1]
    tiles_a = rows_a // ROW_TILE
    assert rows_a % ROW_TILE == 0 and (m - rows_a) % ROW_TILE == 0
    row = lambda i: (i, 0)
    fixed = lambda i: (0, 0)
    return pl.pallas_call(
        functools.partial(_ple_final_body, tiles_a=tiles_a),
        grid=(m // ROW_TILE,),
        in_specs=[
            pl.BlockSpec((ROW_TILE, d), row),
            pl.BlockSpec((ROW_TILE, pd), row),
            pl.BlockSpec((1, d), fixed),
            pl.BlockSpec((pd, d), fixed),
            pl.BlockSpec((d, d), fixed),
            pl.BlockSpec((1, d), fixed),
        ],
        out_specs=[
            pl.BlockSpec((ROW_TILE, d), lambda i: (jnp.minimum(i, tiles_a - 1), 0)),
            pl.BlockSpec((ROW_TILE, d), lambda i: (jnp.maximum(i - tiles_a, 0), 0)),
        ],
        out_shape=[
            jax.ShapeDtypeStruct((rows_a, d), F32),
            jax.ShapeDtypeStruct((m - rows_a, d), F32),
        ],
        compiler_params=_cparams(("arbitrary",)),
        name="ple_final",
    )(x, p, g.reshape(1, d), w_pe, w_pg, g_final.reshape(1, d))


def _div(x, n):
    assert n & (n - 1) == 0
    return lax.shift_right_logical(x, n.bit_length() - 1)


def _chunk_masks(seq_len):
    i = lax.broadcasted_iota(jnp.int32, (CHUNK, CHUNK), 0)
    j = lax.broadcasted_iota(jnp.int32, (CHUNK, CHUNK), 1)
    same = _div(i, seq_len) == _div(j, seq_len)
    incl = same & (i >= j)
    strict = same & (i > j)
    return i, j, same, incl, strict


def _unit_lower_inverse(a, i, j, seq_len):
    base = SUBLANES
    eye = (i == j).astype(F32)
    dg = jnp.where(_div(i, base) == _div(j, base), a, 0.0)
    p1 = _bdot(dg, dg)
    p2 = _bdot(p1, p1)
    y = eye - dg
    y = y + _bdot(y, p1)
    x = y + _bdot(y, p2)
    b = base
    while b < seq_len:
        cm = jnp.where((_div(i, 2 * b) == _div(j, 2 * b)) & (_div(i, b) != _div(j, b)), a, 0.0)
        x = x - _bdot(_bdot(x, cm), x)
        b *= 2
    return x


def _bdot(a, b):
    return lax.dot_general(a.astype(BF16), b.astype(BF16), (((2,), (1,)), ((0,), (0,))),
                           preferred_element_type=F32)


def _bdot_nt(a, b):
    return lax.dot_general(a.astype(BF16), b.astype(BF16), (((2,), (2,)), ((0,), (0,))),
                           preferred_element_type=F32)


def _bdot_tn(a, b):
    return lax.dot_general(a.astype(BF16), b.astype(BF16), (((1,), (1,)), ((0,), (0,))),
                           preferred_element_type=F32)


def _split(a):
    hi = a.astype(BF16)
    return hi, (a - hi.astype(F32)).astype(BF16)


def _bdot3(a, b):
    ah, al = _split(a)
    bh, bl = _split(b)
    return _bdot(ah, bh) + (_bdot(ah, bl) + _bdot(al, bh))


def _heads(x, width, first=0):
    return jnp.stack([x[:, first + h * width:first + (h + 1) * width] for h in range(H_A)])


def _gdn_pre(c, gates, gc_cols, gc_rows, gl_cols, masks, seq_len):
    i, j, same, incl, strict = masks
    q = _heads(c, DK)
    k = _heads(c, DK, H_A * DK)
    v = _heads(c, DV, 2 * H_A * DK)
    q = q * lax.rsqrt(jnp.sum(q * q, axis=-1, keepdims=True) + EPS) * (DK ** -0.5)
    k = k * lax.rsqrt(jnp.sum(k * k, axis=-1, keepdims=True) + EPS)
    gcc = _heads(gc_cols, 1)
    gl = _heads(gl_cols, 1)
    beta = _heads(gates, 1, H_A)
    gcr = jnp.stack([gc_rows[h:h + 1, :] for h in range(H_A)])
    decay = jnp.where(incl, jnp.exp(jnp.where(incl, gcc - gcr, 0.0)), 0.0)
    kb = k * beta
    a = jnp.where(strict, _bdot_nt(kb, k) * decay, 0.0)
    qk = _bdot_nt(q, k) * decay
    x = _unit_lower_inverse(a, i, j, seq_len)
    eg = jnp.exp(gcc)
    rhs = jnp.concatenate([v * beta, kb * eg], axis=2)
    sol0 = _bdot(x, rhs)
    sol = sol0 + _bdot(x, rhs - sol0 - _bdot3(a, sol0))
    u = sol[:, :, :DV]
    w = sol[:, :, DV:]
    q_dec = q * eg
    k_dec = k * jnp.exp(gl - gcc)
    return u, w, qk, q_dec, k_dec, gl


def _gdn_store(o_ref, o, onorm, gate):
    on = o * lax.rsqrt(jnp.mean(o * o, axis=-1, keepdims=True) + EPS) * onorm
    out = on * (gate * _sigmoid(gate))
    for h in range(H_A):
        o_ref[:, h * DV:(h + 1) * DV] = out[h]


def _gdn_prompt_body(proj_ref, gates_ref, gt_ref, cw_ref, on_ref, o_ref, s_out_ref, s_ref, tail_ref):
    n = pl.program_id(1)

    @pl.when(n == 0)
    def _():
        s_ref[...] = jnp.zeros_like(s_ref)
        tail_ref[0:SUBLANES, :] = jnp.zeros((SUBLANES, QKV_DIM), F32)

    x = proj_ref[:, :QKV_DIM]
    tail_ref[SUBLANES:, :] = x
    cw = cw_ref[...]
    conv = x * cw[CONV_W - 1:CONV_W, :]
    for s in range(1, CONV_W):
        conv = conv + tail_ref[SUBLANES - s:SUBLANES - s + CHUNK, :] * cw[CONV_W - 1 - s:CONV_W - s, :]
    tail_ref[0:SUBLANES, :] = x[CHUNK - SUBLANES:, :]
    c = conv * _sigmoid(conv)

    masks = _chunk_masks(CHUNK)
    i, j, same, incl, strict = masks
    gates = gates_ref[...]
    gc_cols = _dot_f32(incl.astype(F32), gates)
    gc_rows = _dot_f32(gt_ref[0], (i <= j).astype(F32))
    gl_cols = jnp.broadcast_to(gc_cols[CHUNK - 1:CHUNK, :], gc_cols.shape)
    u, w, qk, q_dec, k_dec, gl = _gdn_pre(c, gates, gc_cols, gc_rows, gl_cols, masks, CHUNK)
    s = s_ref[...]
    both = _bdot(jnp.concatenate([w, q_dec], axis=1), s)
    v_new = u - both[:, :CHUNK]
    o = both[:, CHUNK:] + _bdot(qk, v_new)
    s_ref[...] = s * jnp.exp(gl[:, 0:1, :]) + _bdot_tn(k_dec, v_new)
    _gdn_store(o_ref, o, on_ref[...], _heads(proj_ref[:, QKV_DIM:], DV))

    @pl.when(n == pl.num_programs(1) - 1)
    def _():
        s_out_ref[0] = s_ref[...]


def _gdn_prompt(proj, gates, gates_t, conv_w, o_norm, bsz, t_len):
    nc = t_len // CHUNK
    n_proj = proj.shape[1]
    row = lambda b, n: (b * nc + n, 0)
    return pl.pallas_call(
        _gdn_prompt_body,
        grid=(bsz, nc),
        in_specs=[
            pl.BlockSpec((CHUNK, n_proj), row),
            pl.BlockSpec((CHUNK, LANES), row),
            pl.BlockSpec((1, 2 * H_A, CHUNK), lambda b, n: (b * nc + n, 0, 0)),
            pl.BlockSpec((CONV_W, QKV_DIM), lambda b, n: (0, 0)),
            pl.BlockSpec((1, DV), lambda b, n: (0, 0)),
        ],
        out_specs=[
            pl.BlockSpec((CHUNK, H_A * DV), row),
            pl.BlockSpec((1, H_A, DK, DV), lambda b, n: (b, 0, 0, 0)),
        ],
        out_shape=[
            jax.ShapeDtypeStruct((bsz * t_len, H_A * DV), F32),
            jax.ShapeDtypeStruct((bsz, H_A, DK, DV), F32),
        ],
        scratch_shapes=[pltpu.VMEM((H_A, DK, DV), F32), pltpu.VMEM((SUBLANES + CHUNK, QKV_DIM), F32)],
        compiler_params=_cparams(("parallel", "arbitrary")),
        name="gdn_prompt",
    )(proj, gates, gates_t, conv_w, o_norm.reshape(1, DV))


def _gdn_sample_body(proj_ref, gates_ref, gt_ref, cw_ref, on_ref, prev_ref, s_in_ref,
                     o_ref, s_out_ref, *, seq_len):
    nseq = CHUNK // seq_len
    x = proj_ref[:, :QKV_DIM]
    prev = prev_ref[...].reshape(CHUNK, QKV_DIM)
    cw = cw_ref[...]
    r = lax.broadcasted_iota(jnp.int32, (CHUNK, 1), 0) & (seq_len - 1)
    conv = x * cw[CONV_W - 1:CONV_W, :]
    for s in range(1, CONV_W):
        shifted = jnp.where(r < s, pltpu.roll(prev, (s - seq_len) % CHUNK, axis=0),
                            pltpu.roll(x, s, axis=0))
        conv = conv + shifted * cw[CONV_W - 1 - s:CONV_W - s, :]
    c = conv * _sigmoid(conv)

    masks = _chunk_masks(seq_len)
    i, j, same, incl, strict = masks
    gates = gates_ref[...]
    gc_cols = _dot_f32(incl.astype(F32), gates)
    gc_rows = _dot_f32(gt_ref[0], (same & (i <= j)).astype(F32))
    gl_cols = _dot_f32(same.astype(F32), gates)
    rows = _div(lax.broadcasted_iota(jnp.int32, (2 * CHUNK, 1), 0) & (CHUNK - 1), seq_len)
    u, w, qk, q_dec, k_dec, gl = _gdn_pre(c, gates, gc_cols, gc_rows, gl_cols, masks, seq_len)
    lhs = jnp.concatenate([w, q_dec], axis=1)
    both = jnp.zeros((H_A, 2 * CHUNK, DV), F32)
    for q in range(nseq):
        both = jnp.where(rows == q, _bdot(lhs, s_in_ref[q]), both)
    v_new = u - both[:, :CHUNK]
    o = both[:, CHUNK:] + _bdot(qk, v_new)
    for q in range(nseq):
        g_tot = jnp.exp(gl[:, q * seq_len:q * seq_len + 1, :])
        s_out_ref[q] = s_in_ref[q] * g_tot + _bdot_tn(jnp.where(rows[:CHUNK] == q, k_dec, 0.0), v_new)
    _gdn_store(o_ref, o, on_ref[...], _heads(proj_ref[:, QKV_DIM:], DV))


def _gdn_sample(proj, gates, gates_t, conv_w, o_norm, prev8, states, layer, row0, bsz, t_len):
    nseq = CHUNK // t_len
    steps = bsz // nseq
    blk0 = row0 // CHUNK
    n_proj = proj.shape[1]
    row = lambda g: (blk0 + g, 0)
    return pl.pallas_call(
        functools.partial(_gdn_sample_body, seq_len=t_len),
        grid=(steps,),
        in_specs=[
            pl.BlockSpec((CHUNK, n_proj), row),
            pl.BlockSpec((CHUNK, LANES), row),
            pl.BlockSpec((1, 2 * H_A, CHUNK), lambda g: (blk0 + g, 0, 0)),
            pl.BlockSpec((CONV_W, QKV_DIM), lambda g: (0, 0)),
            pl.BlockSpec((1, DV), lambda g: (0, 0)),
            pl.BlockSpec((nseq, SUBLANES, QKV_DIM), lambda g: (g, 0, 0)),
            pl.BlockSpec((None, nseq, H_A, DK, DV), lambda g: (layer, g, 0, 0, 0)),
        ],
        out_specs=[
            pl.BlockSpec((CHUNK, H_A * DV), lambda g: (g, 0)),
            pl.BlockSpec((nseq, H_A, DK, DV), lambda g: (g, 0, 0, 0)),
        ],
        out_shape=[
            jax.ShapeDtypeStruct((bsz * t_len, H_A * DV), F32),
            jax.ShapeDtypeStruct((bsz, H_A, DK, DV), F32),
        ],
        compiler_params=_cparams(("parallel",)),
        name="gdn_sample",
    )(proj, gates, gates_t, conv_w, o_norm.reshape(1, DV), prev8, states)


LOG2E = 1.4426950408889634
SB_QSCALE = (HD_B ** -0.5) * LOG2E


def _softplus2(x):
    neg_abs = lax.bitcast_convert_type(
        lax.bitcast_convert_type(x, jnp.uint32) | jnp.uint32(0x80000000), F32)
    return jnp.maximum(x, 0.0) + jnp.log2(1.0 + jnp.exp2(neg_abs))


def _sb_mass(z, mask, u):
    sp = _softplus2(z)
    if mask is not None:
        sp = jnp.where(mask, sp, 0.0)
    return jnp.dot(sp.astype(BF16), u, preferred_element_type=F32)


def _sb_weights(z, mask, csum):
    att = jnp.exp2(z - csum)
    if mask is not None:
        att = jnp.where(mask, att, 0.0)
    return att


def _sb_prompt_body(qt_ref, kt_ref, bias_ref, q_ref, k_ref, v_ref, u_ref, o_ref, acc_ref, run_ref):
    hp = pl.program_id(1)
    p = pl.program_id(2)
    qi = qt_ref[p]
    kj = kt_ref[p]
    heads = LANES // HD_B
    ratio = SB_BQ // SB_BK
    diag = kj >= qi * ratio

    @pl.when(kj == qi * ratio + (ratio - 1))
    def _():
        acc_ref[...] = jnp.zeros_like(acc_ref)
        run_ref[...] = jnp.zeros_like(run_ref)

    q = q_ref[...] * SB_QSCALE
    k = k_ref[...].astype(BF16)
    v = v_ref[...]
    u = u_ref[...]
    lane = _div(lax.broadcasted_iota(jnp.int32, (1, LANES), 1), HD_B)

    def step(masked):
        mask = None
        if masked:
            qpos = lax.broadcasted_iota(jnp.int32, (SB_BQ, SB_BK), 0) + qi * SB_BQ
            kpos = lax.broadcasted_iota(jnp.int32, (SB_BQ, SB_BK), 1) + kj * SB_BK
            mask = kpos < qpos
        acc = acc_ref[...]
        for a in range(heads):
            mine = lane == a
            z = _dot_nt(jnp.where(mine, q, 0.0), k) + bias_ref[hp * heads + a] * LOG2E
            csum = run_ref[a] + _sb_mass(z, mask, u)
            acc = acc + _dot(_sb_weights(z, mask, csum), jnp.where(mine, v, 0.0))
            run_ref[a] = csum[:, 0:1]
        acc_ref[...] = acc

    @pl.when(diag)
    def _():
        step(True)

    @pl.when(jnp.logical_not(diag))
    def _():
        step(False)

    @pl.when(kj == 0)
    def _():
        o_ref[...] = acc_ref[...]


def _sb_prompt(q, k, v, bias, bsz, t_len):
    d = H_B * HD_B
    nq = t_len // SB_BQ
    nk = t_len // SB_BK
    ratio = SB_BQ // SB_BK
    pairs = [(qi, kj) for qi in range(nq) for kj in range(qi * ratio + ratio - 1, -1, -1)]
    qt = jnp.array([pq for pq, _ in pairs], jnp.int32)
    kt = jnp.array([pk for _, pk in pairs], jnp.int32)
    ii = lax.broadcasted_iota(jnp.int32, (SB_BK, SB_BK), 0)
    jj = lax.broadcasted_iota(jnp.int32, (SB_BK, SB_BK), 1)
    u = (ii >= jj).astype(BF16)
    heads = LANES // HD_B
    grid_spec = pltpu.PrefetchScalarGridSpec(
        num_scalar_prefetch=2,
        grid=(bsz, d // LANES, len(pairs)),
        in_specs=[
            pl.BlockSpec(memory_space=pltpu.SMEM),
            pl.BlockSpec((SB_BQ, LANES), lambda b, h, p, qt, kt: (b * nq + qt[p], h)),
            pl.BlockSpec((SB_BK, LANES), lambda b, h, p, qt, kt: (b * nk + kt[p], h)),
            pl.BlockSpec((SB_BK, LANES), lambda b, h, p, qt, kt: (b * nk + kt[p], h)),
            pl.BlockSpec((SB_BK, SB_BK), lambda b, h, p, qt, kt: (0, 0)),
        ],
        out_specs=pl.BlockSpec((SB_BQ, LANES), lambda b, h, p, qt, kt: (b * nq + qt[p], h)),
        scratch_shapes=[pltpu.VMEM((SB_BQ, LANES), F32), pltpu.VMEM((heads, SB_BQ, 1), F32)],
    )
    return pl.pallas_call(
        _sb_prompt_body,
        grid_spec=grid_spec,
        out_shape=jax.ShapeDtypeStruct((bsz * t_len, d), F32),
        compiler_params=_cparams(("parallel", "parallel", "arbitrary")),
        name="sb_prompt",
    )(qt, kt, bias, q, k, v, u)


def _sb_sample_body(pt_ref, bias_ref, q_ref, kn_ref, vn_ref, *rest, t_len):
    kp_refs = rest[:SB_PAGES]
    vp_refs = rest[SB_PAGES:2 * SB_PAGES]
    u_ref, o_ref, qbd_ref, acc_ref, run_ref, bcol_ref = rest[2 * SB_PAGES:]
    s = pl.program_id(1)
    d = H_B * HD_B
    rows = H_B * t_len
    rowh = _div(lax.broadcasted_iota(jnp.int32, (rows, 1), 0), t_len)
    colh = _div(lax.broadcasted_iota(jnp.int32, (1, d), 1), HD_B)
    u = u_ref[...]

    def blocks(kbs, vbs, masks):
        zs = [(_dot_nt(qbd_ref[...], kb) if m is not None else _dot(qbd_ref[...], kb)) + bcol_ref[...]
              for kb, m in zip(kbs, masks)]
        masses = [_sb_mass(z, m, u) for z, m in zip(zs, masks)]
        run = run_ref[...]
        pv = acc_ref[...]
        for z, m, mass, vb in zip(zs, masks, masses, vbs):
            csum = run + mass
            att = _sb_weights(z, m, csum)
            pv = pv + (_dot(att, vb) if m is not None else _dot_nt(att, vb))
            run = csum[:, 0:1]
        acc_ref[...] = pv
        run_ref[...] = run

    pages_k = [r[0] for r in kp_refs]
    pages_v = [r[0] for r in vp_refs]

    @pl.when(s == 0)
    def _():
        q = q_ref[0] * SB_QSCALE
        qbd_ref[...] = jnp.where(rowh == colh, jnp.concatenate([q] * H_B, axis=0), 0.0).astype(BF16)
        bcol = jnp.zeros((rows, 1), F32)
        for h in range(H_B):
            bcol = jnp.where(rowh == h, bias_ref[h] * LOG2E, bcol)
        bcol_ref[...] = bcol
        acc_ref[...] = jnp.zeros_like(acc_ref)
        run_ref[...] = jnp.zeros_like(run_ref)
        pad = jnp.zeros((PAGE - t_len, d), F32)
        t = lax.broadcasted_iota(jnp.int32, (rows, PAGE), 0) & (t_len - 1)
        jk = lax.broadcasted_iota(jnp.int32, (rows, PAGE), 1)
        blocks([jnp.concatenate([kn_ref[0], pad], axis=0)] + pages_k,
               [jnp.concatenate([vn_ref[0], pad], axis=0)] + pages_v,
               [jk < t] + [None] * SB_PAGES)

    @pl.when(s != 0)
    def _():
        blocks(pages_k, pages_v, [None] * SB_PAGES)

    @pl.when(s == pl.num_programs(1) - 1)
    def _():
        g = jnp.where(rowh == colh, acc_ref[...], 0.0)
        o_ref[0] = jnp.sum(g.reshape(H_B, t_len, d), axis=0)


def _sb_sample(q, k_new, v_new, cache_k, cache_v, page_table, bias, bsz, t_len):
    d = H_B * HD_B
    n_pages = page_table.shape[1]
    rows = H_B * t_len
    ii = lax.broadcasted_iota(jnp.int32, (PAGE, PAGE), 0)
    jj = lax.broadcasted_iota(jnp.int32, (PAGE, PAGE), 1)
    u = (ii >= jj).astype(BF16)
    tok = lambda b, s, pt: (b, 0, 0)

    def page(p):
        return lambda b, s, pt: (pt[b * n_pages + (n_pages - 1 - (s * SB_PAGES + p))], 0, 0)

    page_specs = [pl.BlockSpec((1, d, PAGE), page(p)) for p in range(SB_PAGES)]
    grid_spec = pltpu.PrefetchScalarGridSpec(
        num_scalar_prefetch=1,
        grid=(bsz, n_pages // SB_PAGES),
        in_specs=[
            pl.BlockSpec(memory_space=pltpu.SMEM),
            pl.BlockSpec((1, t_len, d), tok),
            pl.BlockSpec((1, t_len, d), tok),
            pl.BlockSpec((1, t_len, d), tok),
            *page_specs,
            *page_specs,
            pl.BlockSpec((PAGE, PAGE), lambda b, s, pt: (0, 0)),
        ],
        out_specs=pl.BlockSpec((1, t_len, d), tok),
        scratch_shapes=[
            pltpu.VMEM((rows, d), BF16),
            pltpu.VMEM((rows, d), F32),
            pltpu.VMEM((rows, 1), F32),
            pltpu.VMEM((rows, 1), F32),
        ],
    )
    out = pl.pallas_call(
        functools.partial(_sb_sample_body, t_len=t_len),
        grid_spec=grid_spec,
        out_shape=jax.ShapeDtypeStruct((bsz, t_len, d), F32),
        compiler_params=_cparams(("parallel", "arbitrary")),
        name="sb_sample",
    )(page_table.reshape(-1), bias, q.reshape(bsz, t_len, d), k_new.reshape(bsz, t_len, d),
      v_new.reshape(bsz, t_len, d), *([cache_k] * SB_PAGES), *([cache_v] * SB_PAGES), u)
    return out.reshape(bsz * t_len, d)


def kernel(x_prompt, x_sample, cache_k, cache_v, state_conv, state_rec, page_table, p_prompt, p_sample, ffn1_norm, ffn1_w_in, ffn1_w_out, mix_norm, ffn2_norm, ffn2_w_in, ffn2_w_out, ple_w_in, ple_norm, ple_w_gate, a_w_in, a_conv, a_A_log, a_dt_bias, a_o_norm, a_w_out, kv_norm, w_kv, b_w_q, b_bias, b_w_o, final_norm):
    bp, tp, d = x_prompt.shape
    bs, ts, _ = x_sample.shape
    depth = ffn1_norm.shape[0]
    n_a = a_w_in.shape[0]
    mp = bp * tp
    ms = bs * ts
    dkv = H_B * HD_B
    n_main = QKV_DIM + H_A * DV

    x = jnp.concatenate([x_prompt.reshape(mp, d), x_sample.reshape(ms, d)], axis=0)
    p_all = jnp.concatenate([p_prompt.reshape(depth, mp, -1), p_sample.reshape(depth, ms, -1)], axis=1)
    ck = cache_k.transpose(0, 2, 3, 1).reshape(cache_k.shape[0], dkv, cache_k.shape[1])
    cv = cache_v.transpose(0, 2, 3, 1).reshape(cache_v.shape[0], dkv, cache_v.shape[1])
    prev8 = jnp.pad(state_conv, ((0, 0), (0, 0), (SUBLANES - (CONV_W - 1), 0), (0, 0)))

    convs_p, convs_s, recs_p, recs_s = [], [], [], []
    k_all = v_all = None
    for i in range(depth):
        x = _ffn(x, ffn1_norm[i], ffn1_w_in[i].astype(BF16), ffn1_w_out[i].astype(BF16))
        if i < n_a:
            proj, gates, gates_t = _gdn_proj(
                x, mix_norm[i], a_w_in[i][:, :n_main].astype(BF16), a_w_in[i][:, n_main:],
                a_A_log[i], a_dt_bias[i])
            gt = gates_t.reshape(2 * H_A, -1, CHUNK).transpose(1, 0, 2)
            o_p, s_p = _gdn_prompt(proj, gates, gt, a_conv[i], a_o_norm[i], bp, tp)
            o_s, s_s = _gdn_sample(proj, gates, gt, a_conv[i], a_o_norm[i], prev8[i], state_rec, i,
                                   mp, bs, ts)
            convs_p.append(jnp.stack([proj[(b + 1) * tp - (CONV_W - 1):(b + 1) * tp, :QKV_DIM]
                                      for b in range(bp)]))
            convs_s.append(proj[mp:, :QKV_DIM].reshape(bs, ts, QKV_DIM)[:, ts - (CONV_W - 1):])
            recs_p.append(s_p)
            recs_s.append(s_s)
            x = _matmul_residual(x, o_p, o_s, a_w_out[i].astype(BF16))
        else:
            ib = i - n_a
            q = _norm_matmul(x, mix_norm[i], b_w_q[ib].astype(BF16))
            o_p = _sb_prompt(q, k_all, v_all, b_bias[ib], bp, tp)
            o_s = _sb_sample(q[mp:], k_all[mp:], v_all[mp:], ck, cv, page_table, b_bias[ib], bs, ts)
            x = _matmul_residual(x, o_p, o_s, b_w_o[ib].astype(BF16))
        x = _ffn(x, ffn2_norm[i], ffn2_w_in[i].astype(BF16), ffn2_w_out[i].astype(BF16))
        ple_args = (x, p_all[i], ple_norm[i], ple_w_in[i].astype(BF16), ple_w_gate[i].astype(BF16))
        if i < depth - 1:
            x = _ple(*ple_args)
        else:
            y_p, y_s = _ple_final(*ple_args, final_norm, mp)
        if i == n_a - 1:
            k_all = _norm_matmul(x, kv_norm, w_kv[:, :dkv].astype(BF16))
            v_all = _norm_matmul(x, kv_norm, w_kv[:, dkv:].astype(BF16))

    y_prompt = y_p.reshape(bp, tp, d)
    y_sample = y_s.reshape(bs, ts, d)
    k_prompt = k_all[:mp].reshape(bp, tp, H_B, HD_B)
    v_prompt = v_all[:mp].reshape(bp, tp, H_B, HD_B)
    k_sample = k_all[mp:].reshape(bs, ts, H_B, HD_B)
    v_sample = v_all[mp:].reshape(bs, ts, H_B, HD_B)
    return (y_prompt, y_sample, jnp.stack(convs_p), jnp.stack(recs_p), k_prompt, v_prompt,
            jnp.stack(convs_s), jnp.stack(recs_s), k_sample, v_sample)
```

```python
import functools

import jax
import jax.numpy as jnp
from jax import lax
from jax.experimental import pallas as pl
from jax.experimental.pallas import tpu as pltpu

F32 = jnp.float32
BF16 = jnp.bfloat16
EPS = 1e-6

H_A = 8
DK = 128
DV = 128
QKV_DIM = H_A * (2 * DK + DV)
CONV_W = 4
CHUNK = 64
GDN_CHUNKS = 4
H_B = 16
HD_B = 64
PAGE = 128

LANES = 128
SUBLANES = 8
VMEM_LIMIT = 48 * 1024 * 1024

ROW_TILE = 1024
FF_TILE = 256
SB_BQ = 1024
SB_BK = 256
SB_W = 256
SB_PAGES = 8


def _cparams(sem):
    return pltpu.CompilerParams(dimension_semantics=sem, vmem_limit_bytes=VMEM_LIMIT)


def _rms(x, g):
    ms = jnp.mean(x * x, axis=-1, keepdims=True)
    return x * lax.rsqrt(ms + EPS) * g


def _sigmoid(x):
    return 1.0 / (1.0 + jnp.exp(-x))


def _softplus(x):
    return jnp.maximum(x, 0.0) + jnp.log1p(jnp.exp(-jnp.abs(x)))


def _dot(a, b):
    return jnp.dot(a.astype(BF16), b.astype(BF16), preferred_element_type=F32)


def _dot_nt(a, b):
    return lax.dot_general(a.astype(BF16), b.astype(BF16), (((1,), (1,)), ((), ())),
                           preferred_element_type=F32)


def _dot_tn(a, b):
    return lax.dot_general(a.astype(BF16), b.astype(BF16), (((0,), (0,)), ((), ())),
                           preferred_element_type=F32)


def _dot_f32(a, b):
    return jnp.dot(a, b, preferred_element_type=F32, precision=lax.Precision.HIGHEST)


def _ffn_body(x_ref, g_ref, wg_ref, wu_ref, wo_ref, o_ref, h_ref):
    @pl.when(pl.program_id(1) == 0)
    def _():
        x = x_ref[...]
        h_ref[...] = _rms(x, g_ref[...]).astype(BF16)
        o_ref[...] = x

    h = h_ref[...]
    gate = jnp.dot(h, wg_ref[...], preferred_element_type=F32)
    up = jnp.dot(h, wu_ref[...], preferred_element_type=F32)
    act = gate * _sigmoid(gate) * up
    o_ref[...] += 0.5 * _dot(act, wo_ref[...])


def _ffn(x, g, w_in, w_out):
    m, d = x.shape
    f = w_out.shape[0]
    nf = f // FF_TILE
    return pl.pallas_call(
        _ffn_body,
        grid=(m // ROW_TILE, nf),
        in_specs=[
            pl.BlockSpec((ROW_TILE, d), lambda i, j: (i, 0)),
            pl.BlockSpec((1, d), lambda i, j: (0, 0)),
            pl.BlockSpec((d, FF_TILE), lambda i, j: (0, j)),
            pl.BlockSpec((d, FF_TILE), lambda i, j: (0, j + nf)),
            pl.BlockSpec((FF_TILE, d), lambda i, j: (j, 0)),
        ],
        out_specs=pl.BlockSpec((ROW_TILE, d), lambda i, j: (i, 0)),
        out_shape=jax.ShapeDtypeStruct((m, d), F32),
        scratch_shapes=[pltpu.VMEM((ROW_TILE, d), BF16)],
        compiler_params=_cparams(("parallel", "arbitrary")),
        name="ffn",
    )(x, g.reshape(1, d), w_in, w_in, w_out)


def _nmm_body(x_ref, g_ref, w_ref, o_ref, h_ref):
    @pl.when(pl.program_id(1) == 0)
    def _():
        h_ref[...] = _rms(x_ref[...], g_ref[...]).astype(BF16)

    o_ref[...] = jnp.dot(h_ref[...], w_ref[...], preferred_element_type=F32)


def _norm_matmul(x, g, w, tn=512):
    m, d = x.shape
    n = w.shape[1]
    return pl.pallas_call(
        _nmm_body,
        grid=(m // ROW_TILE, n // tn),
        in_specs=[
            pl.BlockSpec((ROW_TILE, d), lambda i, j: (i, 0)),
            pl.BlockSpec((1, d), lambda i, j: (0, 0)),
            pl.BlockSpec((d, tn), lambda i, j: (0, j)),
        ],
        out_specs=pl.BlockSpec((ROW_TILE, tn), lambda i, j: (i, j)),
        out_shape=jax.ShapeDtypeStruct((m, n), F32),
        scratch_shapes=[pltpu.VMEM((ROW_TILE, d), BF16)],
        compiler_params=_cparams(("parallel", "arbitrary")),
        name="norm_matmul",
    )(x, g.reshape(1, d), w)


def _nmm_t_body(x_ref, g_ref, wa_ref, wb_ref, oa_ref, ob_ref):
    h = _rms(x_ref[...], g_ref[...]).astype(BF16)
    oa_ref[...] = _dot_nt(wa_ref[...], h)
    ob_ref[...] = _dot_nt(wb_ref[...], h)


def _norm_matmul_t(x, g, wa_t, wb_t, bsz, t_len):
    d = x.shape[1]
    n = wa_t.shape[0]
    nt = t_len // ROW_TILE
    out = pl.BlockSpec((None, n, ROW_TILE), lambda b, i: (b, 0, i))
    return pl.pallas_call(
        _nmm_t_body,
        grid=(bsz, nt),
        in_specs=[
            pl.BlockSpec((ROW_TILE, d), lambda b, i: (b * nt + i, 0)),
            pl.BlockSpec((1, d), lambda b, i: (0, 0)),
            pl.BlockSpec((n, d), lambda b, i: (0, 0)),
            pl.BlockSpec((n, d), lambda b, i: (0, 0)),
        ],
        out_specs=[out, out],
        out_shape=[jax.ShapeDtypeStruct((bsz, n, t_len), F32)] * 2,
        compiler_params=_cparams(("parallel", "parallel")),
        name="norm_matmul_t",
    )(x, g.reshape(1, d), wa_t, wb_t)


def _gdn_proj_body(x_ref, g_ref, w_ref, wab_ref, wabt_ref, prow_ref, pcol_ref,
                   o_ref, gates_ref, gatest_ref, h_ref):
    @pl.when(pl.program_id(1) == 0)
    def _():
        h = _rms(x_ref[...], g_ref[...]).astype(BF16)
        h_ref[...] = h
        ab = jnp.dot(h, wab_ref[...], preferred_element_type=F32)
        lane = lax.broadcasted_iota(jnp.int32, (1, LANES), 1)
        gv = -jnp.exp(prow_ref[0:1, :]) * _softplus(ab + prow_ref[1:2, :])
        gates_ref[...] = jnp.where(lane < H_A, gv, jnp.where(lane < 2 * H_A, _sigmoid(ab), 0.0))
        abt = lax.dot_general(wabt_ref[...], h, (((1,), (1,)), ((), ())),
                              preferred_element_type=F32)
        row = lax.broadcasted_iota(jnp.int32, (2 * H_A, 1), 0)
        gvt = -jnp.exp(pcol_ref[:, 0:1]) * _softplus(abt + pcol_ref[:, 1:2])
        gatest_ref[...] = jnp.where(row < H_A, gvt, _sigmoid(abt))

    o_ref[...] = jnp.dot(h_ref[...], w_ref[...], preferred_element_type=F32)


def _gdn_proj(x, g, w_main, w_ab, a_log, dt_bias, tn=512):
    m, d = x.shape
    n = w_main.shape[1]
    wab = jnp.pad(w_ab, ((0, 0), (0, LANES - 2 * H_A))).astype(BF16)
    wabt = w_ab.T.astype(BF16)
    zeros = jnp.zeros((H_A,), F32)
    prow = jnp.stack([jnp.pad(a_log, (0, LANES - H_A)), jnp.pad(dt_bias, (0, LANES - H_A))])
    prow = jnp.pad(prow, ((0, SUBLANES - 2), (0, 0)))
    pcol = jnp.stack([jnp.concatenate([a_log, zeros]), jnp.concatenate([dt_bias, zeros])], axis=1)
    pcol = jnp.pad(pcol, ((0, 0), (0, LANES - 2)))
    return pl.pallas_call(
        _gdn_proj_body,
        grid=(m // ROW_TILE, n // tn),
        in_specs=[
            pl.BlockSpec((ROW_TILE, d), lambda i, j: (i, 0)),
            pl.BlockSpec((1, d), lambda i, j: (0, 0)),
            pl.BlockSpec((d, tn), lambda i, j: (0, j)),
            pl.BlockSpec((d, LANES), lambda i, j: (0, 0)),
            pl.BlockSpec((2 * H_A, d), lambda i, j: (0, 0)),
            pl.BlockSpec((SUBLANES, LANES), lambda i, j: (0, 0)),
            pl.BlockSpec((2 * H_A, LANES), lambda i, j: (0, 0)),
        ],
        out_specs=[
            pl.BlockSpec((ROW_TILE, tn), lambda i, j: (i, j)),
            pl.BlockSpec((ROW_TILE, LANES), lambda i, j: (i, 0)),
            pl.BlockSpec((2 * H_A, ROW_TILE), lambda i, j: (0, i)),
        ],
        out_shape=[
            jax.ShapeDtypeStruct((m, n), F32),
            jax.ShapeDtypeStruct((m, LANES), F32),
            jax.ShapeDtypeStruct((2 * H_A, m), F32),
        ],
        scratch_shapes=[pltpu.VMEM((ROW_TILE, d), BF16)],
        compiler_params=_cparams(("parallel", "arbitrary")),
        name="gdn_proj",
    )(x, g.reshape(1, d), w_main, wab, wabt, prow, pcol)


def _mm_res_body(x_ref, ya_ref, yb_ref, w_ref, o_ref, *, tiles_a):
    i = pl.program_id(0)

    @pl.when(i < tiles_a)
    def _():
        o_ref[...] = x_ref[...] + _dot(ya_ref[...], w_ref[...])

    @pl.when(i >= tiles_a)
    def _():
        o_ref[...] = x_ref[...] + _dot(yb_ref[...], w_ref[...])


def _matmul_residual(x, ya, yb, w):
    m, d = x.shape
    k = ya.shape[1]
    tiles_a = ya.shape[0] // ROW_TILE
    assert ya.shape[0] % ROW_TILE == 0 and yb.shape[0] % ROW_TILE == 0
    return pl.pallas_call(
        functools.partial(_mm_res_body, tiles_a=tiles_a),
        grid=(m // ROW_TILE,),
        in_specs=[
            pl.BlockSpec((ROW_TILE, d), lambda i: (i, 0)),
            pl.BlockSpec((ROW_TILE, k), lambda i: (jnp.minimum(i, tiles_a - 1), 0)),
            pl.BlockSpec((ROW_TILE, k), lambda i: (jnp.maximum(i - tiles_a, 0), 0)),
            pl.BlockSpec((k, d), lambda i: (0, 0)),
        ],
        out_specs=pl.BlockSpec((ROW_TILE, d), lambda i: (i, 0)),
        out_shape=jax.ShapeDtypeStruct((m, d), F32),
        compiler_params=_cparams(("arbitrary",)),
        name="matmul_residual",
    )(x, ya, yb, w)


def _ple_body(x_ref, p_ref, g_ref, wpe_ref, wpg_ref, o_ref):
    x = x_ref[...]
    gate = _sigmoid(_dot(_rms(x, g_ref[...]), wpg_ref[...]))
    o_ref[...] = x + _dot(p_ref[...], wpe_ref[...]) * gate


def _ple_final_body(x_ref, p_ref, g_ref, wpe_ref, wpg_ref, gf_ref, oa_ref, ob_ref, *, tiles_a):
    x = x_ref[...]
    gate = _sigmoid(_dot(_rms(x, g_ref[...]), wpg_ref[...]))
    y = _rms(x + _dot(p_ref[...], wpe_ref[...]) * gate, gf_ref[...])
    i = pl.program_id(0)

    @pl.when(i < tiles_a)
    def _():
        oa_ref[...] = y

    @pl.when(i >= tiles_a)
    def _():
        ob_ref[...] = y


def _ple(x, p, g, w_pe, w_pg):
    m, d = x.shape
    pd = p.shape[1]
    row = lambda i: (i, 0)
    fixed = lambda i: (0, 0)
    return pl.pallas_call(
        _ple_body,
        grid=(m // ROW_TILE,),
        in_specs=[
            pl.BlockSpec((ROW_TILE, d), row),
            pl.BlockSpec((ROW_TILE, pd), row),
            pl.BlockSpec((1, d), fixed),
            pl.BlockSpec((pd, d), fixed),
            pl.BlockSpec((d, d), fixed),
        ],
        out_specs=pl.BlockSpec((ROW_TILE, d), row),
        out_shape=jax.ShapeDtypeStruct((m, d), F32),
        compiler_params=_cparams(("parallel",)),
        name="ple",
    )(x, p, g.reshape(1, d), w_pe, w_pg)


def _ple_final(x, p, g, w_pe, w_pg, g_final, rows_a):
    m, d = x.shape
    pd = p.shape[1]
    tiles_a = rows_a // ROW_TILE
    assert rows_a % ROW_TILE == 0 and (m - rows_a) % ROW_TILE == 0
    row = lambda i: (i, 0)
    fixed = lambda i: (0, 0)
    return pl.pallas_call(
        functools.partial(_ple_final_body, tiles_a=tiles_a),
        grid=(m // ROW_TILE,),
        in_specs=[
            pl.BlockSpec((ROW_TILE, d), row),
            pl.BlockSpec((ROW_TILE, pd), row),
            pl.BlockSpec((1, d), fixed),
            pl.BlockSpec((pd, d), fixed),
            pl.BlockSpec((d, d), fixed),
            pl.BlockSpec((1, d), fixed),
        ],
        out_specs=[
            pl.BlockSpec((ROW_TILE, d), lambda i: (jnp.minimum(i, tiles_a - 1), 0)),
            pl.BlockSpec((ROW_TILE, d), lambda i: (jnp.maximum(i - tiles_a, 0), 0)),
        ],
        out_shape=[
            jax.ShapeDtypeStruct((rows_a, d), F32),
            jax.ShapeDtypeStruct((m - rows_a, d), F32),
        ],
        compiler_params=_cparams(("arbitrary",)),
        name="ple_final",
    )(x, p, g.reshape(1, d), w_pe, w_pg, g_final.reshape(1, d))


def _div(x, n):
    assert n & (n - 1) == 0
    return lax.shift_right_logical(x, n.bit_length() - 1)


def _chunk_masks(seq_len):
    i = lax.broadcasted_iota(jnp.int32, (CHUNK, CHUNK), 0)
    j = lax.broadcasted_iota(jnp.int32, (CHUNK, CHUNK), 1)
    same = _div(i, seq_len) == _div(j, seq_len)
    incl = same & (i >= j)
    strict = same & (i > j)
    return i, j, same, incl, strict


def _unit_lower_inverse(a, i, j, seq_len):
    base = SUBLANES
    eye = (i == j).astype(F32)
    dg = jnp.where(_div(i, base) == _div(j, base), a, 0.0)
    p1 = _bdot(dg, dg)
    p2 = _bdot(p1, p1)
    y = eye - dg
    y = y + _bdot(y, p1)
    x = y + _bdot(y, p2)
    b = base
    while b < seq_len:
        cm = jnp.where((_div(i, 2 * b) == _div(j, 2 * b)) & (_div(i, b) != _div(j, b)), a, 0.0)
        x = x - _bdot(_bdot(x, cm), x)
        b *= 2
    return x


def _bdot(a, b):
    return lax.dot_general(a.astype(BF16), b.astype(BF16), (((2,), (1,)), ((0,), (0,))),
                           preferred_element_type=F32)


def _bdot_nt(a, b):
    return lax.dot_general(a.astype(BF16), b.astype(BF16), (((2,), (2,)), ((0,), (0,))),
                           preferred_element_type=F32)


def _bdot_tn(a, b):
    return lax.dot_general(a.astype(BF16), b.astype(BF16), (((1,), (1,)), ((0,), (0,))),
                           preferred_element_type=F32)


def _split(a):
    hi = a.astype(BF16)
    return hi, (a - hi.astype(F32)).astype(BF16)


def _bdot3(a, b):
    ah, al = _split(a)
    bh, bl = _split(b)
    return _bdot(ah, bh) + (_bdot(ah, bl) + _bdot(al, bh))


def _heads(x, width, first=0):
    return jnp.stack([x[n * CHUNK:(n + 1) * CHUNK, first + h * width:first + (h + 1) * width]
                      for n in range(x.shape[0] // CHUNK) for h in range(H_A)])


def _gdn_pre(c, gates, gc_cols, gc_rows, gl_cols, masks, seq_len):
    i, j, same, incl, strict = masks
    q = _heads(c, DK)
    k = _heads(c, DK, H_A * DK)
    v = _heads(c, DV, 2 * H_A * DK)
    q = q * lax.rsqrt(jnp.sum(q * q, axis=-1, keepdims=True) + EPS) * (DK ** -0.5)
    k = k * lax.rsqrt(jnp.sum(k * k, axis=-1, keepdims=True) + EPS)
    gcc = _heads(gc_cols, 1)
    gl = _heads(gl_cols, 1)
    beta = _heads(gates, 1, H_A)
    gcr = jnp.stack([g[h:h + 1, :] for g in gc_rows for h in range(H_A)])
    decay = jnp.where(incl, jnp.exp(jnp.where(incl, gcc - gcr, 0.0)), 0.0)
    kb = k * beta
    a = jnp.where(strict, _bdot_nt(kb, k) * decay, 0.0)
    qk = _bdot_nt(q, k) * decay
    x = _unit_lower_inverse(a, i, j, seq_len)
    eg = jnp.exp(gcc)
    rhs = jnp.concatenate([v * beta, kb * eg], axis=2)
    sol0 = _bdot(x, rhs)
    sol = sol0 + _bdot(x, rhs - sol0 - _bdot3(a, sol0))
    u = sol[:, :, :DV]
    w = sol[:, :, DV:]
    q_dec = q * eg
    k_dec = k * jnp.exp(gl - gcc)
    return u, w, qk, q_dec, k_dec, gl


def _gdn_store(o_ref, o, onorm, gate):
    on = o * lax.rsqrt(jnp.mean(o * o, axis=-1, keepdims=True) + EPS) * onorm
    out = on * (gate * _sigmoid(gate))
    for n in range(out.shape[0] // H_A):
        for h in range(H_A):
            o_ref[n * CHUNK:(n + 1) * CHUNK, h * DV:(h + 1) * DV] = out[n * H_A + h]


def _gdn_prompt_body(proj_ref, gates_ref, gt_ref, cw_ref, on_ref, o_ref, s_out_ref, s_ref, tail_ref):
    n = pl.program_id(1)

    @pl.when(n == 0)
    def _():
        s_ref[...] = jnp.zeros_like(s_ref)
        tail_ref[0:SUBLANES, :] = jnp.zeros((SUBLANES, QKV_DIM), F32)

    rows = GDN_CHUNKS * CHUNK
    x = proj_ref[:, :QKV_DIM]
    tail_ref[SUBLANES:, :] = x
    cw = cw_ref[...]
    conv = x * cw[CONV_W - 1:CONV_W, :]
    for s in range(1, CONV_W):
        conv = conv + tail_ref[SUBLANES - s:SUBLANES - s + rows, :] * cw[CONV_W - 1 - s:CONV_W - s, :]
    tail_ref[0:SUBLANES, :] = x[rows - SUBLANES:, :]
    c = conv * _sigmoid(conv)

    masks = _chunk_masks(CHUNK)
    i, j, same, incl, strict = masks
    gates = gates_ref[...]
    lower = incl.astype(F32)
    upper = (i <= j).astype(F32)
    gc_cols = jnp.concatenate([_dot_f32(lower, gates[m * CHUNK:(m + 1) * CHUNK])
                               for m in range(GDN_CHUNKS)], axis=0)
    gc_rows = [_dot_f32(gt_ref[m], upper) for m in range(GDN_CHUNKS)]
    gl_cols = jnp.concatenate(
        [jnp.broadcast_to(gc_cols[(m + 1) * CHUNK - 1:(m + 1) * CHUNK, :], (CHUNK, LANES))
         for m in range(GDN_CHUNKS)], axis=0)
    u, w, qk, q_dec, k_dec, gl = _gdn_pre(c, gates, gc_cols, gc_rows, gl_cols, masks, CHUNK)
    s = s_ref[...]
    outs = []
    for m in range(GDN_CHUNKS):
        hs = slice(m * H_A, (m + 1) * H_A)
        both = _bdot(jnp.concatenate([w[hs], q_dec[hs]], axis=1), s)
        v_new = u[hs] - both[:, :CHUNK]
        outs.append(both[:, CHUNK:] + _bdot(qk[hs], v_new))
        s = s * jnp.exp(gl[hs][:, 0:1, :]) + _bdot_tn(k_dec[hs], v_new)
    s_ref[...] = s
    _gdn_store(o_ref, jnp.concatenate(outs, axis=0), on_ref[...], _heads(proj_ref[:, QKV_DIM:], DV))

    @pl.when(n == pl.num_programs(1) - 1)
    def _():
        s_out_ref[0] = s_ref[...]


def _gdn_prompt(proj, gates, gates_t, conv_w, o_norm, bsz, t_len):
    rows = GDN_CHUNKS * CHUNK
    nc = t_len // rows
    n_proj = proj.shape[1]
    row = lambda b, n: (b * nc + n, 0)
    return pl.pallas_call(
        _gdn_prompt_body,
        grid=(bsz, nc),
        in_specs=[
            pl.BlockSpec((rows, n_proj), row),
            pl.BlockSpec((rows, LANES), row),
            pl.BlockSpec((GDN_CHUNKS, 2 * H_A, CHUNK), lambda b, n: (b * nc + n, 0, 0)),
            pl.BlockSpec((CONV_W, QKV_DIM), lambda b, n: (0, 0)),
            pl.BlockSpec((1, DV), lambda b, n: (0, 0)),
        ],
        out_specs=[
            pl.BlockSpec((rows, H_A * DV), row),
            pl.BlockSpec((1, H_A, DK, DV), lambda b, n: (b, 0, 0, 0)),
        ],
        out_shape=[
            jax.ShapeDtypeStruct((bsz * t_len, H_A * DV), F32),
            jax.ShapeDtypeStruct((bsz, H_A, DK, DV), F32),
        ],
        scratch_shapes=[pltpu.VMEM((H_A, DK, DV), F32), pltpu.VMEM((SUBLANES + rows, QKV_DIM), F32)],
        compiler_params=_cparams(("parallel", "arbitrary")),
        name="gdn_prompt",
    )(proj, gates, gates_t, conv_w, o_norm.reshape(1, DV))


def _gdn_sample_body(proj_ref, gates_ref, gt_ref, cw_ref, on_ref, prev_ref, s_in_ref,
                     o_ref, s_out_ref, *, seq_len):
    nseq = CHUNK // seq_len
    x = proj_ref[:, :QKV_DIM]
    prev = prev_ref[...].reshape(CHUNK, QKV_DIM)
    cw = cw_ref[...]
    r = lax.broadcasted_iota(jnp.int32, (CHUNK, 1), 0) & (seq_len - 1)
    conv = x * cw[CONV_W - 1:CONV_W, :]
    for s in range(1, CONV_W):
        shifted = jnp.where(r < s, pltpu.roll(prev, (s - seq_len) % CHUNK, axis=0),
                            pltpu.roll(x, s, axis=0))
        conv = conv + shifted * cw[CONV_W - 1 - s:CONV_W - s, :]
    c = conv * _sigmoid(conv)

    masks = _chunk_masks(seq_len)
    i, j, same, incl, strict = masks
    gates = gates_ref[...]
    gc_cols = _dot_f32(incl.astype(F32), gates)
    gc_rows = [_dot_f32(gt_ref[0], (same & (i <= j)).astype(F32))]
    gl_cols = _dot_f32(same.astype(F32), gates)
    rows = _div(lax.broadcasted_iota(jnp.int32, (2 * CHUNK, 1), 0) & (CHUNK - 1), seq_len)
    u, w, qk, q_dec, k_dec, gl = _gdn_pre(c, gates, gc_cols, gc_rows, gl_cols, masks, seq_len)
    lhs = jnp.concatenate([w, q_dec], axis=1)
    both = jnp.zeros((H_A, 2 * CHUNK, DV), F32)
    for q in range(nseq):
        both = jnp.where(rows == q, _bdot(lhs, s_in_ref[q]), both)
    v_new = u - both[:, :CHUNK]
    o = both[:, CHUNK:] + _bdot(qk, v_new)
    for q in range(nseq):
        g_tot = jnp.exp(gl[:, q * seq_len:q * seq_len + 1, :])
        s_out_ref[q] = s_in_ref[q] * g_tot + _bdot_tn(jnp.where(rows[:CHUNK] == q, k_dec, 0.0), v_new)
    _gdn_store(o_ref, o, on_ref[...], _heads(proj_ref[:, QKV_DIM:], DV))


def _gdn_sample(proj, gates, gates_t, conv_w, o_norm, prev8, states, layer, row0, bsz, t_len):
    nseq = CHUNK // t_len
    steps = bsz // nseq
    blk0 = row0 // CHUNK
    n_proj = proj.shape[1]
    row = lambda g: (blk0 + g, 0)
    return pl.pallas_call(
        functools.partial(_gdn_sample_body, seq_len=t_len),
        grid=(steps,),
        in_specs=[
            pl.BlockSpec((CHUNK, n_proj), row),
            pl.BlockSpec((CHUNK, LANES), row),
            pl.BlockSpec((1, 2 * H_A, CHUNK), lambda g: (blk0 + g, 0, 0)),
            pl.BlockSpec((CONV_W, QKV_DIM), lambda g: (0, 0)),
            pl.BlockSpec((1, DV), lambda g: (0, 0)),
            pl.BlockSpec((nseq, SUBLANES, QKV_DIM), lambda g: (g, 0, 0)),
            pl.BlockSpec((None, nseq, H_A, DK, DV), lambda g: (layer, g, 0, 0, 0)),
        ],
        out_specs=[
            pl.BlockSpec((CHUNK, H_A * DV), lambda g: (g, 0)),
            pl.BlockSpec((nseq, H_A, DK, DV), lambda g: (g, 0, 0, 0)),
        ],
        out_shape=[
            jax.ShapeDtypeStruct((bsz * t_len, H_A * DV), F32),
            jax.ShapeDtypeStruct((bsz, H_A, DK, DV), F32),
        ],
        compiler_params=_cparams(("parallel",)),
        name="gdn_sample",
    )(proj, gates, gates_t, conv_w, o_norm.reshape(1, DV), prev8, states)


LOG2E = 1.4426950408889634
SB_QSCALE = (HD_B ** -0.5) * LOG2E


def _softplus2(x):
    neg_abs = lax.bitcast_convert_type(
        lax.bitcast_convert_type(x, jnp.uint32) | jnp.uint32(0x80000000), F32)
    return jnp.maximum(x, 0.0) + jnp.log2(1.0 + jnp.exp2(neg_abs))


def _sb_mass(z, mask, u):
    sp = _softplus2(z)
    if mask is not None:
        sp = jnp.where(mask, sp, 0.0)
    return jnp.dot(sp.astype(BF16), u, preferred_element_type=F32)


def _sb_weights(z, mask, csum):
    att = jnp.exp2(z - csum)
    if mask is not None:
        att = jnp.where(mask, att, 0.0)
    return att


def _sb_prompt_body(qt_ref, kt_ref, bias_ref, q_ref, k_ref, v_ref, u_ref, o_ref, acc_ref, run_ref):
    hp = pl.program_id(1)
    p = pl.program_id(2)
    qi = qt_ref[p]
    kj = kt_ref[p]
    heads = SB_W // HD_B
    ratio = SB_BQ // SB_BK
    diag = kj >= qi * ratio

    @pl.when(kj == qi * ratio + (ratio - 1))
    def _():
        acc_ref[...] = jnp.zeros_like(acc_ref)
        run_ref[...] = jnp.zeros_like(run_ref)

    q = q_ref[...] * SB_QSCALE
    k = k_ref[...].astype(BF16)
    v = v_ref[...]
    u = u_ref[...]
    lane = _div(lax.broadcasted_iota(jnp.int32, (1, SB_W), 1), HD_B)
    vrow = _div(lax.broadcasted_iota(jnp.int32, (SB_W, 1), 0), HD_B)

    def step(masked):
        mask = None
        if masked:
            qpos = lax.broadcasted_iota(jnp.int32, (SB_BQ, SB_BK), 0) + qi * SB_BQ
            kpos = lax.broadcasted_iota(jnp.int32, (SB_BQ, SB_BK), 1) + kj * SB_BK
            mask = kpos < qpos
        acc = acc_ref[...]
        for a in range(heads):
            z = _dot(jnp.where(lane == a, q, 0.0), k) + bias_ref[hp * heads + a] * LOG2E
            csum = run_ref[a] + _sb_mass(z, mask, u)
            acc = acc + _dot_nt(_sb_weights(z, mask, csum), jnp.where(vrow == a, v, 0.0))
            run_ref[a] = csum[:, 0:1]
        acc_ref[...] = acc

    @pl.when(diag)
    def _():
        step(True)

    @pl.when(jnp.logical_not(diag))
    def _():
        step(False)

    @pl.when(kj == 0)
    def _():
        o_ref[...] = acc_ref[...]


def _sb_prompt(q, k, v, bias, bsz, t_len):
    d = H_B * HD_B
    nq = t_len // SB_BQ
    ratio = SB_BQ // SB_BK
    pairs = [(qi, kj) for qi in range(nq) for kj in range(qi * ratio + ratio - 1, -1, -1)]
    qt = jnp.array([pq for pq, _ in pairs], jnp.int32)
    kt = jnp.array([pk for _, pk in pairs], jnp.int32)
    ii = lax.broadcasted_iota(jnp.int32, (SB_BK, SB_BK), 0)
    jj = lax.broadcasted_iota(jnp.int32, (SB_BK, SB_BK), 1)
    u = (ii >= jj).astype(BF16)
    heads = SB_W // HD_B
    grid_spec = pltpu.PrefetchScalarGridSpec(
        num_scalar_prefetch=2,
        grid=(bsz, d // SB_W, len(pairs)),
        in_specs=[
            pl.BlockSpec(memory_space=pltpu.SMEM),
            pl.BlockSpec((SB_BQ, SB_W), lambda b, h, p, qt, kt: (b * nq + qt[p], h)),
            pl.BlockSpec((None, SB_W, SB_BK), lambda b, h, p, qt, kt: (b, h, kt[p])),
            pl.BlockSpec((None, SB_W, SB_BK), lambda b, h, p, qt, kt: (b, h, kt[p])),
            pl.BlockSpec((SB_BK, SB_BK), lambda b, h, p, qt, kt: (0, 0)),
        ],
        out_specs=pl.BlockSpec((SB_BQ, SB_W), lambda b, h, p, qt, kt: (b * nq + qt[p], h)),
        scratch_shapes=[pltpu.VMEM((SB_BQ, SB_W), F32), pltpu.VMEM((heads, SB_BQ, 1), F32)],
    )
    return pl.pallas_call(
        _sb_prompt_body,
        grid_spec=grid_spec,
        out_shape=jax.ShapeDtypeStruct((bsz * t_len, d), F32),
        compiler_params=_cparams(("parallel", "parallel", "arbitrary")),
        name="sb_prompt",
    )(qt, kt, bias, q, k, v, u)


def _sb_sample_body(pt_ref, bias_ref, q_ref, kn_ref, vn_ref, *rest, t_len):
    kp_refs = rest[:SB_PAGES]
    vp_refs = rest[SB_PAGES:2 * SB_PAGES]
    u_ref, o_ref, qbd_ref, acc_ref, run_ref, bcol_ref = rest[2 * SB_PAGES:]
    s = pl.program_id(1)
    d = H_B * HD_B
    rows = H_B * t_len
    rowh = _div(lax.broadcasted_iota(jnp.int32, (rows, 1), 0), t_len)
    colh = _div(lax.broadcasted_iota(jnp.int32, (1, d), 1), HD_B)
    u = u_ref[...]

    def blocks(kbs, vbs, masks):
        zs = [(_dot_nt(qbd_ref[...], kb) if m is not None else _dot(qbd_ref[...], kb)) + bcol_ref[...]
              for kb, m in zip(kbs, masks)]
        masses = [_sb_mass(z, m, u) for z, m in zip(zs, masks)]
        run = run_ref[...]
        pv = acc_ref[...]
        for z, m, mass, vb in zip(zs, masks, masses, vbs):
            csum = run + mass
            att = _sb_weights(z, m, csum)
            pv = pv + (_dot(att, vb) if m is not None else _dot_nt(att, vb))
            run = csum[:, 0:1]
        acc_ref[...] = pv
        run_ref[...] = run

    pages_k = [r[0] for r in kp_refs]
    pages_v = [r[0] for r in vp_refs]

    @pl.when(s == 0)
    def _():
        q = q_ref[0] * SB_QSCALE
        qbd_ref[...] = jnp.where(rowh == colh, jnp.concatenate([q] * H_B, axis=0), 0.0).astype(BF16)
        bcol = jnp.zeros((rows, 1), F32)
        for h in range(H_B):
            bcol = jnp.where(rowh == h, bias_ref[h] * LOG2E, bcol)
        bcol_ref[...] = bcol
        acc_ref[...] = jnp.zeros_like(acc_ref)
        run_ref[...] = jnp.zeros_like(run_ref)
        pad = jnp.zeros((PAGE - t_len, d), F32)
        t = lax.broadcasted_iota(jnp.int32, (rows, PAGE), 0) & (t_len - 1)
        jk = lax.broadcasted_iota(jnp.int32, (rows, PAGE), 1)
        blocks([jnp.concatenate([kn_ref[0], pad], axis=0)] + pages_k,
               [jnp.concatenate([vn_ref[0], pad], axis=0)] + pages_v,
               [jk < t] + [None] * SB_PAGES)

    @pl.when(s != 0)
    def _():
        blocks(pages_k, pages_v, [None] * SB_PAGES)

    @pl.when(s == pl.num_programs(1) - 1)
    def _():
        g = jnp.where(rowh == colh, acc_ref[...], 0.0)
        o_ref[0] = jnp.sum(g.reshape(H_B, t_len, d), axis=0)


def _sb_sample(q, k_new, v_new, cache_k, cache_v, page_table, bias, bsz, t_len):
    d = H_B * HD_B
    n_pages = page_table.shape[1]
    rows = H_B * t_len
    ii = lax.broadcasted_iota(jnp.int32, (PAGE, PAGE), 0)
    jj = lax.broadcasted_iota(jnp.int32, (PAGE, PAGE), 1)
    u = (ii >= jj).astype(BF16)
    tok = lambda b, s, pt: (b, 0, 0)

    def page(p):
        return lambda b, s, pt: (pt[b * n_pages + (n_pages - 1 - (s * SB_PAGES + p))], 0, 0)

    page_specs = [pl.BlockSpec((1, d, PAGE), page(p)) for p in range(SB_PAGES)]
    grid_spec = pltpu.PrefetchScalarGridSpec(
        num_scalar_prefetch=1,
        grid=(bsz, n_pages // SB_PAGES),
        in_specs=[
            pl.BlockSpec(memory_space=pltpu.SMEM),
            pl.BlockSpec((1, t_len, d), tok),
            pl.BlockSpec((1, t_len, d), tok),
            pl.BlockSpec((1, t_len, d), tok),
            *page_specs,
            *page_specs,
            pl.BlockSpec((PAGE, PAGE), lambda b, s, pt: (0, 0)),
        ],
        out_specs=pl.BlockSpec((1, t_len, d), tok),
        scratch_shapes=[
            pltpu.VMEM((rows, d), BF16),
            pltpu.VMEM((rows, d), F32),
            pltpu.VMEM((rows, 1), F32),
            pltpu.VMEM((rows, 1), F32),
        ],
    )
    out = pl.pallas_call(
        functools.partial(_sb_sample_body, t_len=t_len),
        grid_spec=grid_spec,
        out_shape=jax.ShapeDtypeStruct((bsz, t_len, d), F32),
        compiler_params=_cparams(("parallel", "arbitrary")),
        name="sb_sample",
    )(page_table.reshape(-1), bias, q.reshape(bsz, t_len, d), k_new.reshape(bsz, t_len, d),
      v_new.reshape(bsz, t_len, d), *([cache_k] * SB_PAGES), *([cache_v] * SB_PAGES), u)
    return out.reshape(bsz * t_len, d)


def kernel(x_prompt, x_sample, cache_k, cache_v, state_conv, state_rec, page_table, p_prompt, p_sample, ffn1_norm, ffn1_w_in, ffn1_w_out, mix_norm, ffn2_norm, ffn2_w_in, ffn2_w_out, ple_w_in, ple_norm, ple_w_gate, a_w_in, a_conv, a_A_log, a_dt_bias, a_o_norm, a_w_out, kv_norm, w_kv, b_w_q, b_bias, b_w_o, final_norm):
    bp, tp, d = x_prompt.shape
    bs, ts, _ = x_sample.shape
    depth = ffn1_norm.shape[0]
    n_a = a_w_in.shape[0]
    mp = bp * tp
    ms = bs * ts
    dkv = H_B * HD_B
    n_main = QKV_DIM + H_A * DV

    x = jnp.concatenate([x_prompt.reshape(mp, d), x_sample.reshape(ms, d)], axis=0)
    p_all = jnp.concatenate([p_prompt.reshape(depth, mp, -1), p_sample.reshape(depth, ms, -1)], axis=1)
    ck = cache_k.transpose(0, 2, 3, 1).reshape(cache_k.shape[0], dkv, cache_k.shape[1])
    cv = cache_v.transpose(0, 2, 3, 1).reshape(cache_v.shape[0], dkv, cache_v.shape[1])
    prev8 = jnp.pad(state_conv, ((0, 0), (0, 0), (SUBLANES - (CONV_W - 1), 0), (0, 0)))

    convs_p, convs_s, recs_p, recs_s = [], [], [], []
    for i in range(depth):
        x = _ffn(x, ffn1_norm[i], ffn1_w_in[i].astype(BF16), ffn1_w_out[i].astype(BF16))
        if i < n_a:
            proj, gates, gates_t = _gdn_proj(
                x, mix_norm[i], a_w_in[i][:, :n_main].astype(BF16), a_w_in[i][:, n_main:],
                a_A_log[i], a_dt_bias[i])
            gt = gates_t.reshape(2 * H_A, -1, CHUNK).transpose(1, 0, 2)
            o_p, s_p = _gdn_prompt(proj, gates, gt, a_conv[i], a_o_norm[i], bp, tp)
            o_s, s_s = _gdn_sample(proj, gates, gt, a_conv[i], a_o_norm[i], prev8[i], state_rec, i,
                                   mp, bs, ts)
            convs_p.append(jnp.stack([proj[(b + 1) * tp - (CONV_W - 1):(b + 1) * tp, :QKV_DIM]
                                      for b in range(bp)]))
            convs_s.append(proj[mp:, :QKV_DIM].reshape(bs, ts, QKV_DIM)[:, ts - (CONV_W - 1):])
            recs_p.append(s_p)
            recs_s.append(s_s)
            x = _matmul_residual(x, o_p, o_s, a_w_out[i].astype(BF16))
        else:
            ib = i - n_a
            q = _norm_matmul(x, mix_norm[i], b_w_q[ib].astype(BF16))
            o_p = _sb_prompt(q, kt_p, vt_p, b_bias[ib], bp, tp)
            o_s = _sb_sample(q[mp:], k_s, v_s, ck, cv, page_table, b_bias[ib], bs, ts)
            x = _matmul_residual(x, o_p, o_s, b_w_o[ib].astype(BF16))
        x = _ffn(x, ffn2_norm[i], ffn2_w_in[i].astype(BF16), ffn2_w_out[i].astype(BF16))
        ple_args = (x, p_all[i], ple_norm[i], ple_w_in[i].astype(BF16), ple_w_gate[i].astype(BF16))
        if i < depth - 1:
            x = _ple(*ple_args)
        else:
            y_p, y_s = _ple_final(*ple_args, final_norm, mp)
        if i == n_a - 1:
            wk = w_kv[:, :dkv].astype(BF16)
            wv = w_kv[:, dkv:].astype(BF16)
            kt_p, vt_p = _norm_matmul_t(x, kv_norm, wk.T, wv.T, bp, tp)
            k_s = _norm_matmul(x[mp:], kv_norm, wk)
            v_s = _norm_matmul(x[mp:], kv_norm, wv)

    y_prompt = y_p.reshape(bp, tp, d)
    y_sample = y_s.reshape(bs, ts, d)
    k_prompt = kt_p.reshape(bp, H_B, HD_B, tp).transpose(0, 3, 1, 2)
    v_prompt = vt_p.reshape(bp, H_B, HD_B, tp).transpose(0, 3, 1, 2)
    k_sample = k_s.reshape(bs, ts, H_B, HD_B)
    v_sample = v_s.reshape(bs, ts, H_B, HD_B)
    return (y_prompt, y_sample, jnp.stack(convs_p), jnp.stack(recs_p), k_prompt, v_prompt,
            jnp.stack(convs_s), jnp.stack(recs_s), k_sample, v_sample)
```

```python
import functools

import jax
import jax.numpy as jnp
from jax import lax
from jax.experimental import pallas as pl
from jax.experimental.pallas import tpu as pltpu

F32 = jnp.float32
BF16 = jnp.bfloat16
EPS = 1e-6

H_A = 8
DK = 128
DV = 128
QKV_DIM = H_A * (2 * DK + DV)
CONV_W = 4
CHUNK = 64
GDN_CHUNKS = 4
H_B = 16
HD_B = 64
PAGE = 128

LANES = 128
SUBLANES = 8
VMEM_LIMIT = 48 * 1024 * 1024

ROW_TILE = 1024
FF_TILE = 256
SB_BQ = 1024
SB_BK = 256
SB_W = 256
SB_PAGES = 8


def _cparams(sem):
    return pltpu.CompilerParams(dimension_semantics=sem, vmem_limit_bytes=VMEM_LIMIT)


def _rms(x, g):
    ms = jnp.mean(x * x, axis=-1, keepdims=True)
    return x * lax.rsqrt(ms + EPS) * g


def _sigmoid(x):
    return 1.0 / (1.0 + jnp.exp(-x))


def _softplus(x):
    return jnp.maximum(x, 0.0) + jnp.log1p(jnp.exp(-jnp.abs(x)))


def _dot(a, b):
    return jnp.dot(a.astype(BF16), b.astype(BF16), preferred_element_type=F32)


def _dot_nt(a, b):
    return lax.dot_general(a.astype(BF16), b.astype(BF16), (((1,), (1,)), ((), ())),
                           preferred_element_type=F32)


def _dot_tn(a, b):
    return lax.dot_general(a.astype(BF16), b.astype(BF16), (((0,), (0,)), ((), ())),
                           preferred_element_type=F32)


def _dot_f32(a, b):
    return jnp.dot(a, b, preferred_element_type=F32, precision=lax.Precision.HIGHEST)


def _ffn_body(x_ref, g_ref, wg_ref, wu_ref, wo_ref, o_ref, h_ref):
    @pl.when(pl.program_id(1) == 0)
    def _():
        x = x_ref[...]
        h_ref[...] = _rms(x, g_ref[...]).astype(BF16)
        o_ref[...] = x

    h = h_ref[...]
    gate = jnp.dot(h, wg_ref[...], preferred_element_type=F32)
    up = jnp.dot(h, wu_ref[...], preferred_element_type=F32)
    act = gate * _sigmoid(gate) * up
    o_ref[...] += 0.5 * _dot(act, wo_ref[...])


def _ffn(x, g, w_in, w_out):
    m, d = x.shape
    f = w_out.shape[0]
    nf = f // FF_TILE
    return pl.pallas_call(
        _ffn_body,
        grid=(m // ROW_TILE, nf),
        in_specs=[
            pl.BlockSpec((ROW_TILE, d), lambda i, j: (i, 0)),
            pl.BlockSpec((1, d), lambda i, j: (0, 0)),
            pl.BlockSpec((d, FF_TILE), lambda i, j: (0, j)),
            pl.BlockSpec((d, FF_TILE), lambda i, j: (0, j + nf)),
            pl.BlockSpec((FF_TILE, d), lambda i, j: (j, 0)),
        ],
        out_specs=pl.BlockSpec((ROW_TILE, d), lambda i, j: (i, 0)),
        out_shape=jax.ShapeDtypeStruct((m, d), F32),
        scratch_shapes=[pltpu.VMEM((ROW_TILE, d), BF16)],
        compiler_params=_cparams(("parallel", "arbitrary")),
        name="ffn",
    )(x, g.reshape(1, d), w_in, w_in, w_out)


def _nmm_body(x_ref, g_ref, w_ref, o_ref, h_ref):
    @pl.when(pl.program_id(1) == 0)
    def _():
        h_ref[...] = _rms(x_ref[...], g_ref[...]).astype(BF16)

    o_ref[...] = jnp.dot(h_ref[...], w_ref[...], preferred_element_type=F32)


def _norm_matmul(x, g, w, tn=1024):
    m, d = x.shape
    n = w.shape[1]
    return pl.pallas_call(
        _nmm_body,
        grid=(m // ROW_TILE, n // tn),
        in_specs=[
            pl.BlockSpec((ROW_TILE, d), lambda i, j: (i, 0)),
            pl.BlockSpec((1, d), lambda i, j: (0, 0)),
            pl.BlockSpec((d, tn), lambda i, j: (0, j)),
        ],
        out_specs=pl.BlockSpec((ROW_TILE, tn), lambda i, j: (i, j)),
        out_shape=jax.ShapeDtypeStruct((m, n), F32),
        scratch_shapes=[pltpu.VMEM((ROW_TILE, d), BF16)],
        compiler_params=_cparams(("parallel", "arbitrary")),
        name="norm_matmul",
    )(x, g.reshape(1, d), w)


def _nmm_t_body(x_ref, g_ref, wa_ref, wb_ref, oa_ref, ob_ref):
    h = _rms(x_ref[...], g_ref[...]).astype(BF16)
    oa_ref[...] = _dot_nt(wa_ref[...], h)
    ob_ref[...] = _dot_nt(wb_ref[...], h)


def _norm_matmul_t(x, g, wa_t, wb_t, bsz, t_len):
    d = x.shape[1]
    n = wa_t.shape[0]
    nt = t_len // ROW_TILE
    out = pl.BlockSpec((None, n, ROW_TILE), lambda b, i: (b, 0, i))
    return pl.pallas_call(
        _nmm_t_body,
        grid=(bsz, nt),
        in_specs=[
            pl.BlockSpec((ROW_TILE, d), lambda b, i: (b * nt + i, 0)),
            pl.BlockSpec((1, d), lambda b, i: (0, 0)),
            pl.BlockSpec((n, d), lambda b, i: (0, 0)),
            pl.BlockSpec((n, d), lambda b, i: (0, 0)),
        ],
        out_specs=[out, out],
        out_shape=[jax.ShapeDtypeStruct((bsz, n, t_len), F32)] * 2,
        compiler_params=_cparams(("parallel", "parallel")),
        name="norm_matmul_t",
    )(x, g.reshape(1, d), wa_t, wb_t)


def _gdn_proj_body(x_ref, g_ref, w_ref, wab_ref, wabt_ref, prow_ref, pcol_ref,
                   o_ref, gates_ref, gatest_ref, h_ref):
    @pl.when(pl.program_id(1) == 0)
    def _():
        h = _rms(x_ref[...], g_ref[...]).astype(BF16)
        h_ref[...] = h
        ab = jnp.dot(h, wab_ref[...], preferred_element_type=F32)
        lane = lax.broadcasted_iota(jnp.int32, (1, LANES), 1)
        gv = -jnp.exp(prow_ref[0:1, :]) * _softplus(ab + prow_ref[1:2, :])
        gates_ref[...] = jnp.where(lane < H_A, gv, jnp.where(lane < 2 * H_A, _sigmoid(ab), 0.0))
        abt = lax.dot_general(wabt_ref[...], h, (((1,), (1,)), ((), ())),
                              preferred_element_type=F32)
        row = lax.broadcasted_iota(jnp.int32, (2 * H_A, 1), 0)
        gvt = -jnp.exp(pcol_ref[:, 0:1]) * _softplus(abt + pcol_ref[:, 1:2])
        gatest_ref[...] = jnp.where(row < H_A, gvt, _sigmoid(abt))

    o_ref[...] = jnp.dot(h_ref[...], w_ref[...], preferred_element_type=F32)


def _gdn_proj(x, g, w_main, w_ab, a_log, dt_bias, tn=2048):
    m, d = x.shape
    n = w_main.shape[1]
    wab = jnp.pad(w_ab, ((0, 0), (0, LANES - 2 * H_A))).astype(BF16)
    wabt = w_ab.T.astype(BF16)
    zeros = jnp.zeros((H_A,), F32)
    prow = jnp.stack([jnp.pad(a_log, (0, LANES - H_A)), jnp.pad(dt_bias, (0, LANES - H_A))])
    prow = jnp.pad(prow, ((0, SUBLANES - 2), (0, 0)))
    pcol = jnp.stack([jnp.concatenate([a_log, zeros]), jnp.concatenate([dt_bias, zeros])], axis=1)
    pcol = jnp.pad(pcol, ((0, 0), (0, LANES - 2)))
    return pl.pallas_call(
        _gdn_proj_body,
        grid=(m // ROW_TILE, n // tn),
        in_specs=[
            pl.BlockSpec((ROW_TILE, d), lambda i, j: (i, 0)),
            pl.BlockSpec((1, d), lambda i, j: (0, 0)),
            pl.BlockSpec((d, tn), lambda i, j: (0, j)),
            pl.BlockSpec((d, LANES), lambda i, j: (0, 0)),
            pl.BlockSpec((2 * H_A, d), lambda i, j: (0, 0)),
            pl.BlockSpec((SUBLANES, LANES), lambda i, j: (0, 0)),
            pl.BlockSpec((2 * H_A, LANES), lambda i, j: (0, 0)),
        ],
        out_specs=[
            pl.BlockSpec((ROW_TILE, tn), lambda i, j: (i, j)),
            pl.BlockSpec((ROW_TILE, LANES), lambda i, j: (i, 0)),
            pl.BlockSpec((2 * H_A, ROW_TILE), lambda i, j: (0, i)),
        ],
        out_shape=[
            jax.ShapeDtypeStruct((m, n), F32),
            jax.ShapeDtypeStruct((m, LANES), F32),
            jax.ShapeDtypeStruct((2 * H_A, m), F32),
        ],
        scratch_shapes=[pltpu.VMEM((ROW_TILE, d), BF16)],
        compiler_params=_cparams(("parallel", "arbitrary")),
        name="gdn_proj",
    )(x, g.reshape(1, d), w_main, wab, wabt, prow, pcol)


def _mm_res_body(x_ref, ya_ref, yb_ref, w_ref, o_ref, *, tiles_a):
    i = pl.program_id(0)

    @pl.when(i < tiles_a)
    def _():
        o_ref[...] = x_ref[...] + _dot(ya_ref[...], w_ref[...])

    @pl.when(i >= tiles_a)
    def _():
        o_ref[...] = x_ref[...] + _dot(yb_ref[...], w_ref[...])


def _matmul_residual(x, ya, yb, w):
    m, d = x.shape
    k = ya.shape[1]
    tiles_a = ya.shape[0] // ROW_TILE
    assert ya.shape[0] % ROW_TILE == 0 and yb.shape[0] % ROW_TILE == 0
    return pl.pallas_call(
        functools.partial(_mm_res_body, tiles_a=tiles_a),
        grid=(m // ROW_TILE,),
        in_specs=[
            pl.BlockSpec((ROW_TILE, d), lambda i: (i, 0)),
            pl.BlockSpec((ROW_TILE, k), lambda i: (jnp.minimum(i, tiles_a - 1), 0)),
            pl.BlockSpec((ROW_TILE, k), lambda i: (jnp.maximum(i - tiles_a, 0), 0)),
            pl.BlockSpec((k, d), lambda i: (0, 0)),
        ],
        out_specs=pl.BlockSpec((ROW_TILE, d), lambda i: (i, 0)),
        out_shape=jax.ShapeDtypeStruct((m, d), F32),
        compiler_params=_cparams(("arbitrary",)),
        name="matmul_residual",
    )(x, ya, yb, w)


def _ple_body(x_ref, p_ref, g_ref, wpe_ref, wpg_ref, o_ref):
    x = x_ref[...]
    gate = _sigmoid(_dot(_rms(x, g_ref[...]), wpg_ref[...]))
    o_ref[...] = x + _dot(p_ref[...], wpe_ref[...]) * gate


def _ple_final_body(x_ref, p_ref, g_ref, wpe_ref, wpg_ref, gf_ref, oa_ref, ob_ref, *, tiles_a):
    x = x_ref[...]
    gate = _sigmoid(_dot(_rms(x, g_ref[...]), wpg_ref[...]))
    y = _rms(x + _dot(p_ref[...], wpe_ref[...]) * gate, gf_ref[...])
    i = pl.program_id(0)

    @pl.when(i < tiles_a)
    def _():
        oa_ref[...] = y

    @pl.when(i >= tiles_a)
    def _():
        ob_ref[...] = y


def _ple(x, p, g, w_pe, w_pg):
    m, d = x.shape
    pd = p.shape[1]
    row = lambda i: (i, 0)
    fixed = lambda i: (0, 0)
    return pl.pallas_call(
        _ple_body,
        grid=(m // ROW_TILE,),
        in_specs=[
            pl.BlockSpec((ROW_TILE, d), row),
            pl.BlockSpec((ROW_TILE, pd), row),
            pl.BlockSpec((1, d), fixed),
            pl.BlockSpec((pd, d), fixed),
            pl.BlockSpec((d, d), fixed),
        ],
        out_specs=pl.BlockSpec((ROW_TILE, d), row),
        out_shape=jax.ShapeDtypeStruct((m, d), F32),
        compiler_params=_cparams(("parallel",)),
        name="ple",
    )(x, p, g.reshape(1, d), w_pe, w_pg)


def _ple_final(x, p, g, w_pe, w_pg, g_final, rows_a):
    m, d = x.shape
    pd = p.shape[1]
    tiles_a = rows_a // ROW_TILE
    assert rows_a % ROW_TILE == 0 and (m - rows_a) % ROW_TILE == 0
    row = lambda i: (i, 0)
    fixed = lambda i: (0, 0)
    return pl.pallas_call(
        functools.partial(_ple_final_body, tiles_a=tiles_a),
        grid=(m // ROW_TILE,),
        in_specs=[
            pl.BlockSpec((ROW_TILE, d), row),
            pl.BlockSpec((ROW_TILE, pd), row),
            pl.BlockSpec((1, d), fixed),
            pl.BlockSpec((pd, d), fixed),
            pl.BlockSpec((d, d), fixed),
            pl.BlockSpec((1, d), fixed),
        ],
        out_specs=[
            pl.BlockSpec((ROW_TILE, d), lambda i: (jnp.minimum(i, tiles_a - 1), 0)),
            pl.BlockSpec((ROW_TILE, d), lambda i: (jnp.maximum(i - tiles_a, 0), 0)),
        ],
        out_shape=[
            jax.ShapeDtypeStruct((rows_a, d), F32),
            jax.ShapeDtypeStruct((m - rows_a, d), F32),
        ],
        compiler_params=_cparams(("arbitrary",)),
        name="ple_final",
    )(x, p, g.reshape(1, d), w_pe, w_pg, g_final.reshape(1, d))


def _div(x, n):
    assert n & (n - 1) == 0
    return lax.shift_right_logical(x, n.bit_length() - 1)


def _chunk_masks(seq_len):
    i = lax.broadcasted_iota(jnp.int32, (CHUNK, CHUNK), 0)
    j = lax.broadcasted_iota(jnp.int32, (CHUNK, CHUNK), 1)
    same = _div(i, seq_len) == _div(j, seq_len)
    incl = same & (i >= j)
    strict = same & (i > j)
    return i, j, same, incl, strict


def _unit_lower_inverse(a, i, j, seq_len):
    base = SUBLANES
    eye = (i == j).astype(F32)
    dg = jnp.where(_div(i, base) == _div(j, base), a, 0.0)
    p1 = _bdot(dg, dg)
    p2 = _bdot(p1, p1)
    y = eye - dg
    y = y + _bdot(y, p1)
    x = y + _bdot(y, p2)
    b = base
    while b < seq_len:
        cm = jnp.where((_div(i, 2 * b) == _div(j, 2 * b)) & (_div(i, b) != _div(j, b)), a, 0.0)
        x = x - _bdot(_bdot(x, cm), x)
        b *= 2
    return x


def _bdot(a, b):
    return lax.dot_general(a.astype(BF16), b.astype(BF16), (((2,), (1,)), ((0,), (0,))),
                           preferred_element_type=F32)


def _bdot_nt(a, b):
    return lax.dot_general(a.astype(BF16), b.astype(BF16), (((2,), (2,)), ((0,), (0,))),
                           preferred_element_type=F32)


def _bdot_tn(a, b):
    return lax.dot_general(a.astype(BF16), b.astype(BF16), (((1,), (1,)), ((0,), (0,))),
                           preferred_element_type=F32)


def _split(a):
    hi = a.astype(BF16)
    return hi, (a - hi.astype(F32)).astype(BF16)


def _bdot3(a, b):
    ah, al = _split(a)
    bh, bl = _split(b)
    return _bdot(ah, bh) + (_bdot(ah, bl) + _bdot(al, bh))


def _heads(x, width, first=0):
    return jnp.stack([x[n * CHUNK:(n + 1) * CHUNK, first + h * width:first + (h + 1) * width]
                      for n in range(x.shape[0] // CHUNK) for h in range(H_A)])


def _gdn_pre(c, gates, gc_cols, gc_rows, gl_cols, masks, seq_len):
    i, j, same, incl, strict = masks
    q = _heads(c, DK)
    k = _heads(c, DK, H_A * DK)
    v = _heads(c, DV, 2 * H_A * DK)
    q = q * lax.rsqrt(jnp.sum(q * q, axis=-1, keepdims=True) + EPS) * (DK ** -0.5)
    k = k * lax.rsqrt(jnp.sum(k * k, axis=-1, keepdims=True) + EPS)
    gcc = _heads(gc_cols, 1)
    gl = _heads(gl_cols, 1)
    beta = _heads(gates, 1, H_A)
    gcr = jnp.stack([g[h:h + 1, :] for g in gc_rows for h in range(H_A)])
    decay = jnp.where(incl, jnp.exp(jnp.where(incl, gcc - gcr, 0.0)), 0.0)
    kb = k * beta
    a = jnp.where(strict, _bdot_nt(kb, k) * decay, 0.0)
    qk = _bdot_nt(q, k) * decay
    x = _unit_lower_inverse(a, i, j, seq_len)
    eg = jnp.exp(gcc)
    rhs = jnp.concatenate([v * beta, kb * eg], axis=2)
    sol0 = _bdot(x, rhs)
    sol = sol0 + _bdot(x, rhs - sol0 - _bdot3(a, sol0))
    u = sol[:, :, :DV]
    w = sol[:, :, DV:]
    q_dec = q * eg
    k_dec = k * jnp.exp(gl - gcc)
    return u, w, qk, q_dec, k_dec, gl


def _gdn_store(o_ref, o, onorm, gate):
    on = o * lax.rsqrt(jnp.mean(o * o, axis=-1, keepdims=True) + EPS) * onorm
    out = on * (gate * _sigmoid(gate))
    for n in range(out.shape[0] // H_A):
        for h in range(H_A):
            o_ref[n * CHUNK:(n + 1) * CHUNK, h * DV:(h + 1) * DV] = out[n * H_A + h]


def _gdn_prompt_body(proj_ref, gates_ref, gt_ref, cw_ref, on_ref, o_ref, s_out_ref, s_ref, tail_ref):
    n = pl.program_id(1)

    @pl.when(n == 0)
    def _():
        s_ref[...] = jnp.zeros_like(s_ref)
        tail_ref[0:SUBLANES, :] = jnp.zeros((SUBLANES, QKV_DIM), F32)

    rows = GDN_CHUNKS * CHUNK
    x = proj_ref[:, :QKV_DIM]
    tail_ref[SUBLANES:, :] = x
    cw = cw_ref[...]
    conv = x * cw[CONV_W - 1:CONV_W, :]
    for s in range(1, CONV_W):
        conv = conv + tail_ref[SUBLANES - s:SUBLANES - s + rows, :] * cw[CONV_W - 1 - s:CONV_W - s, :]
    tail_ref[0:SUBLANES, :] = x[rows - SUBLANES:, :]
    c = conv * _sigmoid(conv)

    masks = _chunk_masks(CHUNK)
    i, j, same, incl, strict = masks
    gates = gates_ref[...]
    lower = incl.astype(F32)
    upper = (i <= j).astype(F32)
    gc_cols = jnp.concatenate([_dot_f32(lower, gates[m * CHUNK:(m + 1) * CHUNK])
                               for m in range(GDN_CHUNKS)], axis=0)
    gc_rows = [_dot_f32(gt_ref[m], upper) for m in range(GDN_CHUNKS)]
    gl_cols = jnp.concatenate(
        [jnp.broadcast_to(gc_cols[(m + 1) * CHUNK - 1:(m + 1) * CHUNK, :], (CHUNK, LANES))
         for m in range(GDN_CHUNKS)], axis=0)
    u, w, qk, q_dec, k_dec, gl = _gdn_pre(c, gates, gc_cols, gc_rows, gl_cols, masks, CHUNK)
    s = s_ref[...]
    outs = []
    for m in range(GDN_CHUNKS):
        hs = slice(m * H_A, (m + 1) * H_A)
        both = _bdot(jnp.concatenate([w[hs], q_dec[hs]], axis=1), s)
        v_new = u[hs] - both[:, :CHUNK]
        outs.append(both[:, CHUNK:] + _bdot(qk[hs], v_new))
        s = s * jnp.exp(gl[hs][:, 0:1, :]) + _bdot_tn(k_dec[hs], v_new)
    s_ref[...] = s
    _gdn_store(o_ref, jnp.concatenate(outs, axis=0), on_ref[...], _heads(proj_ref[:, QKV_DIM:], DV))

    @pl.when(n == pl.num_programs(1) - 1)
    def _():
        s_out_ref[0] = s_ref[...]


def _gdn_prompt(proj, gates, gates_t, conv_w, o_norm, bsz, t_len):
    rows = GDN_CHUNKS * CHUNK
    nc = t_len // rows
    n_proj = proj.shape[1]
    row = lambda b, n: (b * nc + n, 0)
    return pl.pallas_call(
        _gdn_prompt_body,
        grid=(bsz, nc),
        in_specs=[
            pl.BlockSpec((rows, n_proj), row),
            pl.BlockSpec((rows, LANES), row),
            pl.BlockSpec((GDN_CHUNKS, 2 * H_A, CHUNK), lambda b, n: (b * nc + n, 0, 0)),
            pl.BlockSpec((CONV_W, QKV_DIM), lambda b, n: (0, 0)),
            pl.BlockSpec((1, DV), lambda b, n: (0, 0)),
        ],
        out_specs=[
            pl.BlockSpec((rows, H_A * DV), row),
            pl.BlockSpec((1, H_A, DK, DV), lambda b, n: (b, 0, 0, 0)),
        ],
        out_shape=[
            jax.ShapeDtypeStruct((bsz * t_len, H_A * DV), F32),
            jax.ShapeDtypeStruct((bsz, H_A, DK, DV), F32),
        ],
        scratch_shapes=[pltpu.VMEM((H_A, DK, DV), F32), pltpu.VMEM((SUBLANES + rows, QKV_DIM), F32)],
        compiler_params=_cparams(("parallel", "arbitrary")),
        name="gdn_prompt",
    )(proj, gates, gates_t, conv_w, o_norm.reshape(1, DV))


def _gdn_sample_body(proj_ref, gates_ref, gt_ref, cw_ref, on_ref, prev_ref, s_in_ref,
                     o_ref, s_out_ref, *, seq_len):
    nseq = CHUNK // seq_len
    x = proj_ref[:, :QKV_DIM]
    prev = prev_ref[...].reshape(CHUNK, QKV_DIM)
    cw = cw_ref[...]
    r = lax.broadcasted_iota(jnp.int32, (CHUNK, 1), 0) & (seq_len - 1)
    conv = x * cw[CONV_W - 1:CONV_W, :]
    for s in range(1, CONV_W):
        shifted = jnp.where(r < s, pltpu.roll(prev, (s - seq_len) % CHUNK, axis=0),
                            pltpu.roll(x, s, axis=0))
        conv = conv + shifted * cw[CONV_W - 1 - s:CONV_W - s, :]
    c = conv * _sigmoid(conv)

    masks = _chunk_masks(seq_len)
    i, j, same, incl, strict = masks
    gates = gates_ref[...]
    gc_cols = _dot_f32(incl.astype(F32), gates)
    gc_rows = [_dot_f32(gt_ref[0], (same & (i <= j)).astype(F32))]
    gl_cols = _dot_f32(same.astype(F32), gates)
    rows = _div(lax.broadcasted_iota(jnp.int32, (2 * CHUNK, 1), 0) & (CHUNK - 1), seq_len)
    u, w, qk, q_dec, k_dec, gl = _gdn_pre(c, gates, gc_cols, gc_rows, gl_cols, masks, seq_len)
    lhs = jnp.concatenate([w, q_dec], axis=1)
    both = jnp.zeros((H_A, 2 * CHUNK, DV), F32)
    for q in range(nseq):
        both = jnp.where(rows == q, _bdot(lhs, s_in_ref[q]), both)
    v_new = u - both[:, :CHUNK]
    o = both[:, CHUNK:] + _bdot(qk, v_new)
    for q in range(nseq):
        g_tot = jnp.exp(gl[:, q * seq_len:q * seq_len + 1, :])
        s_out_ref[q] = s_in_ref[q] * g_tot + _bdot_tn(jnp.where(rows[:CHUNK] == q, k_dec, 0.0), v_new)
    _gdn_store(o_ref, o, on_ref[...], _heads(proj_ref[:, QKV_DIM:], DV))


def _gdn_sample(proj, gates, gates_t, conv_w, o_norm, prev8, states, layer, row0, bsz, t_len):
    nseq = CHUNK // t_len
    steps = bsz // nseq
    blk0 = row0 // CHUNK
    n_proj = proj.shape[1]
    row = lambda g: (blk0 + g, 0)
    return pl.pallas_call(
        functools.partial(_gdn_sample_body, seq_len=t_len),
        grid=(steps,),
        in_specs=[
            pl.BlockSpec((CHUNK, n_proj), row),
            pl.BlockSpec((CHUNK, LANES), row),
            pl.BlockSpec((1, 2 * H_A, CHUNK), lambda g: (blk0 + g, 0, 0)),
            pl.BlockSpec((CONV_W, QKV_DIM), lambda g: (0, 0)),
            pl.BlockSpec((1, DV), lambda g: (0, 0)),
            pl.BlockSpec((nseq, SUBLANES, QKV_DIM), lambda g: (g, 0, 0)),
            pl.BlockSpec((None, nseq, H_A, DK, DV), lambda g: (layer, g, 0, 0, 0)),
        ],
        out_specs=[
            pl.BlockSpec((CHUNK, H_A * DV), lambda g: (g, 0)),
            pl.BlockSpec((nseq, H_A, DK, DV), lambda g: (g, 0, 0, 0)),
        ],
        out_shape=[
            jax.ShapeDtypeStruct((bsz * t_len, H_A * DV), F32),
            jax.ShapeDtypeStruct((bsz, H_A, DK, DV), F32),
        ],
        compiler_params=_cparams(("parallel",)),
        name="gdn_sample",
    )(proj, gates, gates_t, conv_w, o_norm.reshape(1, DV), prev8, states)


LOG2E = 1.4426950408889634
SB_QSCALE = (HD_B ** -0.5) * LOG2E


def _softplus2(x):
    neg_abs = lax.bitcast_convert_type(
        lax.bitcast_convert_type(x, jnp.uint32) | jnp.uint32(0x80000000), F32)
    return jnp.maximum(x, 0.0) + jnp.log2(1.0 + jnp.exp2(neg_abs))


def _sb_mass(z, mask, u):
    sp = _softplus2(z)
    if mask is not None:
        sp = jnp.where(mask, sp, 0.0)
    return jnp.dot(sp.astype(BF16), u, preferred_element_type=F32)


def _sb_weights(z, mask, csum):
    att = jnp.exp2(z - csum)
    if mask is not None:
        att = jnp.where(mask, att, 0.0)
    return att


def _sb_prompt_body(qt_ref, kt_ref, bias_ref, q_ref, k_ref, v_ref, u_ref, o_ref, acc_ref, run_ref):
    hp = pl.program_id(1)
    p = pl.program_id(2)
    qi = qt_ref[p]
    kj = kt_ref[p]
    heads = SB_W // HD_B
    ratio = SB_BQ // SB_BK
    diag = kj >= qi * ratio

    @pl.when(kj == qi * ratio + (ratio - 1))
    def _():
        acc_ref[...] = jnp.zeros_like(acc_ref)
        run_ref[...] = jnp.zeros_like(run_ref)

    q = q_ref[...] * SB_QSCALE
    k = k_ref[...].astype(BF16)
    v = v_ref[...]
    u = u_ref[...]
    lane = _div(lax.broadcasted_iota(jnp.int32, (1, SB_W), 1), HD_B)
    vrow = _div(lax.broadcasted_iota(jnp.int32, (SB_W, 1), 0), HD_B)

    def step(masked, lo):
        mask = None
        if masked:
            qpos = lax.broadcasted_iota(jnp.int32, (SB_BQ - lo, SB_BK), 0) + (qi * SB_BQ + lo)
            kpos = lax.broadcasted_iota(jnp.int32, (SB_BQ - lo, SB_BK), 1) + kj * SB_BK
            mask = kpos < qpos
        acc = acc_ref[lo:, :]
        for a in range(heads):
            z = _dot(jnp.where(lane == a, q[lo:], 0.0), k) + bias_ref[hp * heads + a] * LOG2E
            csum = run_ref[a, lo:] + _sb_mass(z, mask, u)
            acc = acc + _dot_nt(_sb_weights(z, mask, csum), jnp.where(vrow == a, v, 0.0))
            run_ref[a, lo:] = csum[:, 0:1]
        acc_ref[lo:, :] = acc

    for c in range(ratio):
        @pl.when(kj == qi * ratio + c)
        def _():
            step(True, c * SB_BK)

    @pl.when(jnp.logical_not(diag))
    def _():
        step(False, 0)

    @pl.when(kj == 0)
    def _():
        o_ref[...] = acc_ref[...]


def _sb_prompt(q, k, v, bias, bsz, t_len):
    d = H_B * HD_B
    nq = t_len // SB_BQ
    ratio = SB_BQ // SB_BK
    pairs = [(qi, kj) for qi in range(nq) for kj in range(qi * ratio + ratio - 1, -1, -1)]
    qt = jnp.array([pq for pq, _ in pairs], jnp.int32)
    kt = jnp.array([pk for _, pk in pairs], jnp.int32)
    ii = lax.broadcasted_iota(jnp.int32, (SB_BK, SB_BK), 0)
    jj = lax.broadcasted_iota(jnp.int32, (SB_BK, SB_BK), 1)
    u = (ii >= jj).astype(BF16)
    heads = SB_W // HD_B
    grid_spec = pltpu.PrefetchScalarGridSpec(
        num_scalar_prefetch=2,
        grid=(bsz, d // SB_W, len(pairs)),
        in_specs=[
            pl.BlockSpec(memory_space=pltpu.SMEM),
            pl.BlockSpec((SB_BQ, SB_W), lambda b, h, p, qt, kt: (b * nq + qt[p], h)),
            pl.BlockSpec((None, SB_W, SB_BK), lambda b, h, p, qt, kt: (b, h, kt[p])),
            pl.BlockSpec((None, SB_W, SB_BK), lambda b, h, p, qt, kt: (b, h, kt[p])),
            pl.BlockSpec((SB_BK, SB_BK), lambda b, h, p, qt, kt: (0, 0)),
        ],
        out_specs=pl.BlockSpec((SB_BQ, SB_W), lambda b, h, p, qt, kt: (b * nq + qt[p], h)),
        scratch_shapes=[pltpu.VMEM((SB_BQ, SB_W), F32), pltpu.VMEM((heads, SB_BQ, 1), F32)],
    )
    return pl.pallas_call(
        _sb_prompt_body,
        grid_spec=grid_spec,
        out_shape=jax.ShapeDtypeStruct((bsz * t_len, d), F32),
        compiler_params=_cparams(("parallel", "parallel", "arbitrary")),
        name="sb_prompt",
    )(qt, kt, bias, q, k, v, u)


def _sb_sample_body(pt_ref, bias_ref, q_ref, kn_ref, vn_ref, *rest, t_len):
    kp_refs = rest[:SB_PAGES]
    vp_refs = rest[SB_PAGES:2 * SB_PAGES]
    u_ref, o_ref, qbd_ref, acc_ref, run_ref, bcol_ref = rest[2 * SB_PAGES:]
    s = pl.program_id(1)
    d = H_B * HD_B
    rows = H_B * t_len
    rowh = _div(lax.broadcasted_iota(jnp.int32, (rows, 1), 0), t_len)
    colh = _div(lax.broadcasted_iota(jnp.int32, (1, d), 1), HD_B)
    u = u_ref[...]

    def blocks(kbs, vbs, masks):
        zs = [(_dot_nt(qbd_ref[...], kb) if m is not None else _dot(qbd_ref[...], kb)) + bcol_ref[...]
              for kb, m in zip(kbs, masks)]
        masses = [_sb_mass(z, m, u) for z, m in zip(zs, masks)]
        run = run_ref[...]
        pv = acc_ref[...]
        for z, m, mass, vb in zip(zs, masks, masses, vbs):
            csum = run + mass
            att = _sb_weights(z, m, csum)
            pv = pv + (_dot(att, vb) if m is not None else _dot_nt(att, vb))
            run = csum[:, 0:1]
        acc_ref[...] = pv
        run_ref[...] = run

    pages_k = [r[0] for r in kp_refs]
    pages_v = [r[0] for r in vp_refs]

    @pl.when(s == 0)
    def _():
        q = q_ref[0] * SB_QSCALE
        qbd_ref[...] = jnp.where(rowh == colh, jnp.concatenate([q] * H_B, axis=0), 0.0).astype(BF16)
        bcol = jnp.zeros((rows, 1), F32)
        for h in range(H_B):
            bcol = jnp.where(rowh == h, bias_ref[h] * LOG2E, bcol)
        bcol_ref[...] = bcol
        acc_ref[...] = jnp.zeros_like(acc_ref)
        run_ref[...] = jnp.zeros_like(run_ref)
        pad = jnp.zeros((PAGE - t_len, d), F32)
        t = lax.broadcasted_iota(jnp.int32, (rows, PAGE), 0) & (t_len - 1)
        jk = lax.broadcasted_iota(jnp.int32, (rows, PAGE), 1)
        blocks([jnp.concatenate([kn_ref[0], pad], axis=0)] + pages_k,
               [jnp.concatenate([vn_ref[0], pad], axis=0)] + pages_v,
               [jk < t] + [None] * SB_PAGES)

    @pl.when(s != 0)
    def _():
        blocks(pages_k, pages_v, [None] * SB_PAGES)

    @pl.when(s == pl.num_programs(1) - 1)
    def _():
        g = jnp.where(rowh == colh, acc_ref[...], 0.0)
        o_ref[0] = jnp.sum(g.reshape(H_B, t_len, d), axis=0)


def _sb_sample(q, k_new, v_new, cache_k, cache_v, page_table, bias, bsz, t_len):
    d = H_B * HD_B
    n_pages = page_table.shape[1]
    rows = H_B * t_len
    ii = lax.broadcasted_iota(jnp.int32, (PAGE, PAGE), 0)
    jj = lax.broadcasted_iota(jnp.int32, (PAGE, PAGE), 1)
    u = (ii >= jj).astype(BF16)
    tok = lambda b, s, pt: (b, 0, 0)

    def page(p):
        return lambda b, s, pt: (pt[b * n_pages + (n_pages - 1 - (s * SB_PAGES + p))], 0, 0)

    page_specs = [pl.BlockSpec((1, d, PAGE), page(p)) for p in range(SB_PAGES)]
    grid_spec = pltpu.PrefetchScalarGridSpec(
        num_scalar_prefetch=1,
        grid=(bsz, n_pages // SB_PAGES),
        in_specs=[
            pl.BlockSpec(memory_space=pltpu.SMEM),
            pl.BlockSpec((1, t_len, d), tok),
            pl.BlockSpec((1, t_len, d), tok),
            pl.BlockSpec((1, t_len, d), tok),
            *page_specs,
            *page_specs,
            pl.BlockSpec((PAGE, PAGE), lambda b, s, pt: (0, 0)),
        ],
        out_specs=pl.BlockSpec((1, t_len, d), tok),
        scratch_shapes=[
            pltpu.VMEM((rows, d), BF16),
            pltpu.VMEM((rows, d), F32),
            pltpu.VMEM((rows, 1), F32),
            pltpu.VMEM((rows, 1), F32),
        ],
    )
    out = pl.pallas_call(
        functools.partial(_sb_sample_body, t_len=t_len),
        grid_spec=grid_spec,
        out_shape=jax.ShapeDtypeStruct((bsz, t_len, d), F32),
        compiler_params=_cparams(("parallel", "arbitrary")),
        name="sb_sample",
    )(page_table.reshape(-1), bias, q.reshape(bsz, t_len, d), k_new.reshape(bsz, t_len, d),
      v_new.reshape(bsz, t_len, d), *([cache_k] * SB_PAGES), *([cache_v] * SB_PAGES), u)
    return out.reshape(bsz * t_len, d)


def kernel(x_prompt, x_sample, cache_k, cache_v, state_conv, state_rec, page_table, p_prompt, p_sample, ffn1_norm, ffn1_w_in, ffn1_w_out, mix_norm, ffn2_norm, ffn2_w_in, ffn2_w_out, ple_w_in, ple_norm, ple_w_gate, a_w_in, a_conv, a_A_log, a_dt_bias, a_o_norm, a_w_out, kv_norm, w_kv, b_w_q, b_bias, b_w_o, final_norm):
    bp, tp, d = x_prompt.shape
    bs, ts, _ = x_sample.shape
    depth = ffn1_norm.shape[0]
    n_a = a_w_in.shape[0]
    mp = bp * tp
    ms = bs * ts
    dkv = H_B * HD_B
    n_main = QKV_DIM + H_A * DV

    x = jnp.concatenate([x_prompt.reshape(mp, d), x_sample.reshape(ms, d)], axis=0)
    p_all = jnp.concatenate([p_prompt.reshape(depth, mp, -1), p_sample.reshape(depth, ms, -1)], axis=1)
    ck = cache_k.transpose(0, 2, 3, 1).reshape(cache_k.shape[0], dkv, cache_k.shape[1])
    cv = cache_v.transpose(0, 2, 3, 1).reshape(cache_v.shape[0], dkv, cache_v.shape[1])
    prev8 = jnp.pad(state_conv, ((0, 0), (0, 0), (SUBLANES - (CONV_W - 1), 0), (0, 0)))

    convs_p, convs_s, recs_p, recs_s = [], [], [], []
    for i in range(depth):
        x = _ffn(x, ffn1_norm[i], ffn1_w_in[i].astype(BF16), ffn1_w_out[i].astype(BF16))
        if i < n_a:
            proj, gates, gates_t = _gdn_proj(
                x, mix_norm[i], a_w_in[i][:, :n_main].astype(BF16), a_w_in[i][:, n_main:],
                a_A_log[i], a_dt_bias[i])
            gt = gates_t.reshape(2 * H_A, -1, CHUNK).transpose(1, 0, 2)
            o_p, s_p = _gdn_prompt(proj, gates, gt, a_conv[i], a_o_norm[i], bp, tp)
            o_s, s_s = _gdn_sample(proj, gates, gt, a_conv[i], a_o_norm[i], prev8[i], state_rec, i,
                                   mp, bs, ts)
            convs_p.append(jnp.stack([proj[(b + 1) * tp - (CONV_W - 1):(b + 1) * tp, :QKV_DIM]
                                      for b in range(bp)]))
            convs_s.append(proj[mp:, :QKV_DIM].reshape(bs, ts, QKV_DIM)[:, ts - (CONV_W - 1):])
            recs_p.append(s_p)
            recs_s.append(s_s)
            x = _matmul_residual(x, o_p, o_s, a_w_out[i].astype(BF16))
        else:
            ib = i - n_a
            q = _norm_matmul(x, mix_norm[i], b_w_q[ib].astype(BF16))
            o_p = _sb_prompt(q, kt_p, vt_p, b_bias[ib], bp, tp)
            o_s = _sb_sample(q[mp:], k_s, v_s, ck, cv, page_table, b_bias[ib], bs, ts)
            x = _matmul_residual(x, o_p, o_s, b_w_o[ib].astype(BF16))
        x = _ffn(x, ffn2_norm[i], ffn2_w_in[i].astype(BF16), ffn2_w_out[i].astype(BF16))
        ple_args = (x, p_all[i], ple_norm[i], ple_w_in[i].astype(BF16), ple_w_gate[i].astype(BF16))
        if i < depth - 1:
            x = _ple(*ple_args)
        else:
            y_p, y_s = _ple_final(*ple_args, final_norm, mp)
        if i == n_a - 1:
            wk = w_kv[:, :dkv].astype(BF16)
            wv = w_kv[:, dkv:].astype(BF16)
            kt_p, vt_p = _norm_matmul_t(x, kv_norm, wk.T, wv.T, bp, tp)
            k_s = _norm_matmul(x[mp:], kv_norm, wk)
            v_s = _norm_matmul(x[mp:], kv_norm, wv)

    y_prompt = y_p.reshape(bp, tp, d)
    y_sample = y_s.reshape(bs, ts, d)
    k_prompt = kt_p.reshape(bp, H_B, HD_B, tp).transpose(0, 3, 1, 2)
    v_prompt = vt_p.reshape(bp, H_B, HD_B, tp).transpose(0, 3, 1, 2)
    k_sample = k_s.reshape(bs, ts, H_B, HD_B)
    v_sample = v_s.reshape(bs, ts, H_B, HD_B)
    return (y_prompt, y_sample, jnp.stack(convs_p), jnp.stack(recs_p), k_prompt, v_prompt,
            jnp.stack(convs_s), jnp.stack(recs_s), k_sample, v_sample)
```

```python
import functools

import jax
import jax.numpy as jnp
from jax import lax
from jax.experimental import pallas as pl
from jax.experimental.pallas import tpu as pltpu

F32 = jnp.float32
BF16 = jnp.bfloat16
EPS = 1e-6

H_A = 8
DK = 128
DV = 128
QKV_DIM = H_A * (2 * DK + DV)
CONV_W = 4
CHUNK = 64
GDN_CHUNKS = 4
H_B = 16
HD_B = 64
PAGE = 128

LANES = 128
SUBLANES = 8
VMEM_LIMIT = 48 * 1024 * 1024

ROW_TILE = 1024
FF_TILE = 256
SB_BQ = 1024
SB_BK = 256
SB_W = 256
SB_PAGES = 8


def _cparams(sem):
    return pltpu.CompilerParams(dimension_semantics=sem, vmem_limit_bytes=VMEM_LIMIT)


def _rms(x, g):
    ms = jnp.mean(x * x, axis=-1, keepdims=True)
    return x * lax.rsqrt(ms + EPS) * g


def _sigmoid(x):
    return 1.0 / (1.0 + jnp.exp(-x))


def _softplus(x):
    return jnp.maximum(x, 0.0) + jnp.log1p(jnp.exp(-jnp.abs(x)))


def _dot(a, b):
    return jnp.dot(a.astype(BF16), b.astype(BF16), preferred_element_type=F32)


def _dot_nt(a, b):
    return lax.dot_general(a.astype(BF16), b.astype(BF16), (((1,), (1,)), ((), ())),
                           preferred_element_type=F32)


def _dot_tn(a, b):
    return lax.dot_general(a.astype(BF16), b.astype(BF16), (((0,), (0,)), ((), ())),
                           preferred_element_type=F32)


def _dot_f32(a, b):
    return jnp.dot(a, b, preferred_element_type=F32, precision=lax.Precision.HIGHEST)


def _ffn_body(x_ref, g_ref, wg_ref, wu_ref, wo_ref, o_ref, h_ref):
    @pl.when(pl.program_id(1) == 0)
    def _():
        x = x_ref[...]
        h_ref[...] = _rms(x, g_ref[...]).astype(BF16)
        o_ref[...] = x

    h = h_ref[...]
    gate = jnp.dot(h, wg_ref[...], preferred_element_type=F32)
    up = jnp.dot(h, wu_ref[...], preferred_element_type=F32)
    act = gate * _sigmoid(gate) * up
    o_ref[...] += 0.5 * _dot(act, wo_ref[...])


def _ffn(x, g, w_in, w_out):
    m, d = x.shape
    f = w_out.shape[0]
    nf = f // FF_TILE
    return pl.pallas_call(
        _ffn_body,
        grid=(m // ROW_TILE, nf),
        in_specs=[
            pl.BlockSpec((ROW_TILE, d), lambda i, j: (i, 0)),
            pl.BlockSpec((1, d), lambda i, j: (0, 0)),
            pl.BlockSpec((d, FF_TILE), lambda i, j: (0, j)),
            pl.BlockSpec((d, FF_TILE), lambda i, j: (0, j + nf)),
            pl.BlockSpec((FF_TILE, d), lambda i, j: (j, 0)),
        ],
        out_specs=pl.BlockSpec((ROW_TILE, d), lambda i, j: (i, 0)),
        out_shape=jax.ShapeDtypeStruct((m, d), F32),
        scratch_shapes=[pltpu.VMEM((ROW_TILE, d), BF16)],
        compiler_params=_cparams(("parallel", "arbitrary")),
        name="ffn",
    )(x, g.reshape(1, d), w_in, w_in, w_out)


def _nmm_body(x_ref, g_ref, w_ref, o_ref, h_ref):
    @pl.when(pl.program_id(1) == 0)
    def _():
        h_ref[...] = _rms(x_ref[...], g_ref[...]).astype(BF16)

    o_ref[...] = jnp.dot(h_ref[...], w_ref[...], preferred_element_type=F32)


def _norm_matmul(x, g, w, tn=1024):
    m, d = x.shape
    n = w.shape[1]
    return pl.pallas_call(
        _nmm_body,
        grid=(m // ROW_TILE, n // tn),
        in_specs=[
            pl.BlockSpec((ROW_TILE, d), lambda i, j: (i, 0)),
            pl.BlockSpec((1, d), lambda i, j: (0, 0)),
            pl.BlockSpec((d, tn), lambda i, j: (0, j)),
        ],
        out_specs=pl.BlockSpec((ROW_TILE, tn), lambda i, j: (i, j)),
        out_shape=jax.ShapeDtypeStruct((m, n), F32),
        scratch_shapes=[pltpu.VMEM((ROW_TILE, d), BF16)],
        compiler_params=_cparams(("parallel", "arbitrary")),
        name="norm_matmul",
    )(x, g.reshape(1, d), w)


def _nmm_t_body(x_ref, g_ref, wa_ref, wb_ref, oa_ref, ob_ref):
    h = _rms(x_ref[...], g_ref[...]).astype(BF16)
    oa_ref[...] = _dot_nt(wa_ref[...], h)
    ob_ref[...] = _dot_nt(wb_ref[...], h)


def _norm_matmul_t(x, g, wa_t, wb_t, bsz, t_len):
    d = x.shape[1]
    n = wa_t.shape[0]
    nt = t_len // ROW_TILE
    out = pl.BlockSpec((None, n, ROW_TILE), lambda b, i: (b, 0, i))
    return pl.pallas_call(
        _nmm_t_body,
        grid=(bsz, nt),
        in_specs=[
            pl.BlockSpec((ROW_TILE, d), lambda b, i: (b * nt + i, 0)),
            pl.BlockSpec((1, d), lambda b, i: (0, 0)),
            pl.BlockSpec((n, d), lambda b, i: (0, 0)),
            pl.BlockSpec((n, d), lambda b, i: (0, 0)),
        ],
        out_specs=[out, out],
        out_shape=[jax.ShapeDtypeStruct((bsz, n, t_len), F32)] * 2,
        compiler_params=_cparams(("parallel", "parallel")),
        name="norm_matmul_t",
    )(x, g.reshape(1, d), wa_t, wb_t)


def _gdn_proj_body(x_ref, g_ref, w_ref, wab_ref, wabt_ref, prow_ref, pcol_ref,
                   o_ref, gates_ref, gatest_ref, h_ref):
    @pl.when(pl.program_id(1) == 0)
    def _():
        h = _rms(x_ref[...], g_ref[...]).astype(BF16)
        h_ref[...] = h
        ab = jnp.dot(h, wab_ref[...], preferred_element_type=F32)
        lane = lax.broadcasted_iota(jnp.int32, (1, LANES), 1)
        gv = -jnp.exp(prow_ref[0:1, :]) * _softplus(ab + prow_ref[1:2, :])
        gates_ref[...] = jnp.where(lane < H_A, gv, jnp.where(lane < 2 * H_A, _sigmoid(ab), 0.0))
        abt = lax.dot_general(wabt_ref[...], h, (((1,), (1,)), ((), ())),
                              preferred_element_type=F32)
        row = lax.broadcasted_iota(jnp.int32, (2 * H_A, 1), 0)
        gvt = -jnp.exp(pcol_ref[:, 0:1]) * _softplus(abt + pcol_ref[:, 1:2])
        gatest_ref[...] = jnp.where(row < H_A, gvt, _sigmoid(abt))

    o_ref[...] = jnp.dot(h_ref[...], w_ref[...], preferred_element_type=F32)


def _gdn_proj(x, g, w_main, w_ab, a_log, dt_bias, tn=2048):
    m, d = x.shape
    n = w_main.shape[1]
    wab = jnp.pad(w_ab, ((0, 0), (0, LANES - 2 * H_A))).astype(BF16)
    wabt = w_ab.T.astype(BF16)
    zeros = jnp.zeros((H_A,), F32)
    prow = jnp.stack([jnp.pad(a_log, (0, LANES - H_A)), jnp.pad(dt_bias, (0, LANES - H_A))])
    prow = jnp.pad(prow, ((0, SUBLANES - 2), (0, 0)))
    pcol = jnp.stack([jnp.concatenate([a_log, zeros]), jnp.concatenate([dt_bias, zeros])], axis=1)
    pcol = jnp.pad(pcol, ((0, 0), (0, LANES - 2)))
    return pl.pallas_call(
        _gdn_proj_body,
        grid=(m // ROW_TILE, n // tn),
        in_specs=[
            pl.BlockSpec((ROW_TILE, d), lambda i, j: (i, 0)),
            pl.BlockSpec((1, d), lambda i, j: (0, 0)),
            pl.BlockSpec((d, tn), lambda i, j: (0, j)),
            pl.BlockSpec((d, LANES), lambda i, j: (0, 0)),
            pl.BlockSpec((2 * H_A, d), lambda i, j: (0, 0)),
            pl.BlockSpec((SUBLANES, LANES), lambda i, j: (0, 0)),
            pl.BlockSpec((2 * H_A, LANES), lambda i, j: (0, 0)),
        ],
        out_specs=[
            pl.BlockSpec((ROW_TILE, tn), lambda i, j: (i, j)),
            pl.BlockSpec((ROW_TILE, LANES), lambda i, j: (i, 0)),
            pl.BlockSpec((2 * H_A, ROW_TILE), lambda i, j: (0, i)),
        ],
        out_shape=[
            jax.ShapeDtypeStruct((m, n), F32),
            jax.ShapeDtypeStruct((m, LANES), F32),
            jax.ShapeDtypeStruct((2 * H_A, m), F32),
        ],
        scratch_shapes=[pltpu.VMEM((ROW_TILE, d), BF16)],
        compiler_params=_cparams(("parallel", "arbitrary")),
        name="gdn_proj",
    )(x, g.reshape(1, d), w_main, wab, wabt, prow, pcol)


def _mm_res_body(x_ref, ya_ref, yb_ref, w_ref, o_ref, *, tiles_a):
    i = pl.program_id(0)

    @pl.when(i < tiles_a)
    def _():
        o_ref[...] = x_ref[...] + _dot(ya_ref[...], w_ref[...])

    @pl.when(i >= tiles_a)
    def _():
        o_ref[...] = x_ref[...] + _dot(yb_ref[...], w_ref[...])


def _matmul_residual(x, ya, yb, w):
    m, d = x.shape
    k = ya.shape[1]
    tiles_a = ya.shape[0] // ROW_TILE
    assert ya.shape[0] % ROW_TILE == 0 and yb.shape[0] % ROW_TILE == 0
    return pl.pallas_call(
        functools.partial(_mm_res_body, tiles_a=tiles_a),
        grid=(m // ROW_TILE,),
        in_specs=[
            pl.BlockSpec((ROW_TILE, d), lambda i: (i, 0)),
            pl.BlockSpec((ROW_TILE, k), lambda i: (jnp.minimum(i, tiles_a - 1), 0)),
            pl.BlockSpec((ROW_TILE, k), lambda i: (jnp.maximum(i - tiles_a, 0), 0)),
            pl.BlockSpec((k, d), lambda i: (0, 0)),
        ],
        out_specs=pl.BlockSpec((ROW_TILE, d), lambda i: (i, 0)),
        out_shape=jax.ShapeDtypeStruct((m, d), F32),
        compiler_params=_cparams(("arbitrary",)),
        name="matmul_residual",
    )(x, ya, yb, w)


def _ple_body(x_ref, p_ref, g_ref, wpe_ref, wpg_ref, o_ref):
    x = x_ref[...]
    gate = _sigmoid(_dot(_rms(x, g_ref[...]), wpg_ref[...]))
    o_ref[...] = x + _dot(p_ref[...], wpe_ref[...]) * gate


def _ple_final_body(x_ref, p_ref, g_ref, wpe_ref, wpg_ref, gf_ref, oa_ref, ob_ref, *, tiles_a):
    x = x_ref[...]
    gate = _sigmoid(_dot(_rms(x, g_ref[...]), wpg_ref[...]))
    y = _rms(x + _dot(p_ref[...], wpe_ref[...]) * gate, gf_ref[...])
    i = pl.program_id(0)

    @pl.when(i < tiles_a)
    def _():
        oa_ref[...] = y

    @pl.when(i >= tiles_a)
    def _():
        ob_ref[...] = y


def _ple(x, p, g, w_pe, w_pg):
    m, d = x.shape
    pd = p.shape[1]
    row = lambda i: (i, 0)
    fixed = lambda i: (0, 0)
    return pl.pallas_call(
        _ple_body,
        grid=(m // ROW_TILE,),
        in_specs=[
            pl.BlockSpec((ROW_TILE, d), row),
            pl.BlockSpec((ROW_TILE, pd), row),
            pl.BlockSpec((1, d), fixed),
            pl.BlockSpec((pd, d), fixed),
            pl.BlockSpec((d, d), fixed),
        ],
        out_specs=pl.BlockSpec((ROW_TILE, d), row),
        out_shape=jax.ShapeDtypeStruct((m, d), F32),
        compiler_params=_cparams(("parallel",)),
        name="ple",
    )(x, p, g.reshape(1, d), w_pe, w_pg)


def _ple_final(x, p, g, w_pe, w_pg, g_final, rows_a):
    m, d = x.shape
    pd = p.shape[1]
    tiles_a = rows_a // ROW_TILE
    assert rows_a % ROW_TILE == 0 and (m - rows_a) % ROW_TILE == 0
    row = lambda i: (i, 0)
    fixed = lambda i: (0, 0)
    return pl.pallas_call(
        functools.partial(_ple_final_body, tiles_a=tiles_a),
        grid=(m // ROW_TILE,),
        in_specs=[
            pl.BlockSpec((ROW_TILE, d), row),
            pl.BlockSpec((ROW_TILE, pd), row),
            pl.BlockSpec((1, d), fixed),
            pl.BlockSpec((pd, d), fixed),
            pl.BlockSpec((d, d), fixed),
            pl.BlockSpec((1, d), fixed),
        ],
        out_specs=[
            pl.BlockSpec((ROW_TILE, d), lambda i: (jnp.minimum(i, tiles_a - 1), 0)),
            pl.BlockSpec((ROW_TILE, d), lambda i: (jnp.maximum(i - tiles_a, 0), 0)),
        ],
        out_shape=[
            jax.ShapeDtypeStruct((rows_a, d), F32),
            jax.ShapeDtypeStruct((m - rows_a, d), F32),
        ],
        compiler_params=_cparams(("arbitrary",)),
        name="ple_final",
    )(x, p, g.reshape(1, d), w_pe, w_pg, g_final.reshape(1, d))


def _div(x, n):
    assert n & (n - 1) == 0
    return lax.shift_right_logical(x, n.bit_length() - 1)


def _chunk_masks(seq_len):
    i = lax.broadcasted_iota(jnp.int32, (CHUNK, CHUNK), 0)
    j = lax.broadcasted_iota(jnp.int32, (CHUNK, CHUNK), 1)
    same = _div(i, seq_len) == _div(j, seq_len)
    incl = same & (i >= j)
    strict = same & (i > j)
    return i, j, same, incl, strict


def _unit_lower_inverse(a, i, j, seq_len):
    base = SUBLANES
    eye = (i == j).astype(F32)
    dg = jnp.where(_div(i, base) == _div(j, base), a, 0.0)
    p1 = _bdot(dg, dg)
    p2 = _bdot(p1, p1)
    y = eye - dg
    y = y + _bdot(y, p1)
    x = y + _bdot(y, p2)
    b = base
    while b < seq_len:
        cm = jnp.where((_div(i, 2 * b) == _div(j, 2 * b)) & (_div(i, b) != _div(j, b)), a, 0.0)
        x = x - _bdot(_bdot(x, cm), x)
        b *= 2
    return x


def _bdot(a, b):
    return lax.dot_general(a.astype(BF16), b.astype(BF16), (((2,), (1,)), ((0,), (0,))),
                           preferred_element_type=F32)


def _bdot_nt(a, b):
    return lax.dot_general(a.astype(BF16), b.astype(BF16), (((2,), (2,)), ((0,), (0,))),
                           preferred_element_type=F32)


def _bdot_tn(a, b):
    return lax.dot_general(a.astype(BF16), b.astype(BF16), (((1,), (1,)), ((0,), (0,))),
                           preferred_element_type=F32)


def _split(a):
    hi = a.astype(BF16)
    return hi, (a - hi.astype(F32)).astype(BF16)


def _bdot3(a, b):
    ah, al = _split(a)
    bh, bl = _split(b)
    return _bdot(ah, bh) + (_bdot(ah, bl) + _bdot(al, bh))


def _heads(x, width, first=0):
    return jnp.stack([x[n * CHUNK:(n + 1) * CHUNK, first + h * width:first + (h + 1) * width]
                      for n in range(x.shape[0] // CHUNK) for h in range(H_A)])


def _gdn_pre(c, gates, gc_cols, gc_rows, gl_cols, masks, seq_len):
    i, j, same, incl, strict = masks
    q = _heads(c, DK)
    k = _heads(c, DK, H_A * DK)
    v = _heads(c, DV, 2 * H_A * DK)
    q = q * lax.rsqrt(jnp.sum(q * q, axis=-1, keepdims=True) + EPS) * (DK ** -0.5)
    k = k * lax.rsqrt(jnp.sum(k * k, axis=-1, keepdims=True) + EPS)
    gcc = _heads(gc_cols, 1)
    gl = _heads(gl_cols, 1)
    beta = _heads(gates, 1, H_A)
    gcr = jnp.stack([g[h:h + 1, :] for g in gc_rows for h in range(H_A)])
    decay = jnp.where(incl, jnp.exp(jnp.where(incl, gcc - gcr, 0.0)), 0.0)
    kb = k * beta
    a = jnp.where(strict, _bdot_nt(kb, k) * decay, 0.0)
    qk = _bdot_nt(q, k) * decay
    x = _unit_lower_inverse(a, i, j, seq_len)
    eg = jnp.exp(gcc)
    rhs = jnp.concatenate([v * beta, kb * eg], axis=2)
    sol0 = _bdot(x, rhs)
    sol = sol0 + _bdot(x, rhs - sol0 - _bdot3(a, sol0))
    u = sol[:, :, :DV]
    w = sol[:, :, DV:]
    q_dec = q * eg
    k_dec = k * jnp.exp(gl - gcc)
    return u, w, qk, q_dec, k_dec, gl


def _gdn_store(o_ref, o, onorm, gate):
    on = o * lax.rsqrt(jnp.mean(o * o, axis=-1, keepdims=True) + EPS) * onorm
    out = on * (gate * _sigmoid(gate))
    for n in range(out.shape[0] // H_A):
        for h in range(H_A):
            o_ref[n * CHUNK:(n + 1) * CHUNK, h * DV:(h + 1) * DV] = out[n * H_A + h]


def _gdn_prompt_body(proj_ref, gates_ref, gt_ref, cw_ref, on_ref, o_ref, s_out_ref, s_ref, tail_ref):
    n = pl.program_id(1)

    @pl.when(n == 0)
    def _():
        s_ref[...] = jnp.zeros_like(s_ref)
        tail_ref[0:SUBLANES, :] = jnp.zeros((SUBLANES, QKV_DIM), F32)

    rows = GDN_CHUNKS * CHUNK
    x = proj_ref[:, :QKV_DIM]
    tail_ref[SUBLANES:, :] = x
    cw = cw_ref[...]
    conv = x * cw[CONV_W - 1:CONV_W, :]
    for s in range(1, CONV_W):
        conv = conv + tail_ref[SUBLANES - s:SUBLANES - s + rows, :] * cw[CONV_W - 1 - s:CONV_W - s, :]
    tail_ref[0:SUBLANES, :] = x[rows - SUBLANES:, :]
    c = conv * _sigmoid(conv)

    masks = _chunk_masks(CHUNK)
    i, j, same, incl, strict = masks
    gates = gates_ref[...]
    lower = incl.astype(F32)
    upper = (i <= j).astype(F32)
    gc_cols = jnp.concatenate([_dot_f32(lower, gates[m * CHUNK:(m + 1) * CHUNK])
                               for m in range(GDN_CHUNKS)], axis=0)
    gc_rows = [_dot_f32(gt_ref[m], upper) for m in range(GDN_CHUNKS)]
    gl_cols = jnp.concatenate(
        [jnp.broadcast_to(gc_cols[(m + 1) * CHUNK - 1:(m + 1) * CHUNK, :], (CHUNK, LANES))
         for m in range(GDN_CHUNKS)], axis=0)
    u, w, qk, q_dec, k_dec, gl = _gdn_pre(c, gates, gc_cols, gc_rows, gl_cols, masks, CHUNK)
    s = s_ref[...]
    outs = []
    for m in range(GDN_CHUNKS):
        hs = slice(m * H_A, (m + 1) * H_A)
        both = _bdot(jnp.concatenate([w[hs], q_dec[hs]], axis=1), s)
        v_new = u[hs] - both[:, :CHUNK]
        outs.append(both[:, CHUNK:] + _bdot(qk[hs], v_new))
        s = s * jnp.exp(gl[hs][:, 0:1, :]) + _bdot_tn(k_dec[hs], v_new)
    s_ref[...] = s
    _gdn_store(o_ref, jnp.concatenate(outs, axis=0), on_ref[...], _heads(proj_ref[:, QKV_DIM:], DV))

    @pl.when(n == pl.num_programs(1) - 1)
    def _():
        s_out_ref[0] = s_ref[...]


def _gdn_prompt(proj, gates, gates_t, conv_w, o_norm, bsz, t_len):
    rows = GDN_CHUNKS * CHUNK
    nc = t_len // rows
    n_proj = proj.shape[1]
    row = lambda b, n: (b * nc + n, 0)
    return pl.pallas_call(
        _gdn_prompt_body,
        grid=(bsz, nc),
        in_specs=[
            pl.BlockSpec((rows, n_proj), row),
            pl.BlockSpec((rows, LANES), row),
            pl.BlockSpec((GDN_CHUNKS, 2 * H_A, CHUNK), lambda b, n: (b * nc + n, 0, 0)),
            pl.BlockSpec((CONV_W, QKV_DIM), lambda b, n: (0, 0)),
            pl.BlockSpec((1, DV), lambda b, n: (0, 0)),
        ],
        out_specs=[
            pl.BlockSpec((rows, H_A * DV), row),
            pl.BlockSpec((1, H_A, DK, DV), lambda b, n: (b, 0, 0, 0)),
        ],
        out_shape=[
            jax.ShapeDtypeStruct((bsz * t_len, H_A * DV), F32),
            jax.ShapeDtypeStruct((bsz, H_A, DK, DV), F32),
        ],
        scratch_shapes=[pltpu.VMEM((H_A, DK, DV), F32), pltpu.VMEM((SUBLANES + rows, QKV_DIM), F32)],
        compiler_params=_cparams(("parallel", "arbitrary")),
        name="gdn_prompt",
    )(proj, gates, gates_t, conv_w, o_norm.reshape(1, DV))


def _gdn_sample_body(proj_ref, gates_ref, gt_ref, cw_ref, on_ref, prev_ref, s_in_ref,
                     o_ref, s_out_ref, *, seq_len):
    nseq = CHUNK // seq_len
    x = proj_ref[:, :QKV_DIM]
    prev = prev_ref[...].reshape(CHUNK, QKV_DIM)
    cw = cw_ref[...]
    r = lax.broadcasted_iota(jnp.int32, (CHUNK, 1), 0) & (seq_len - 1)
    conv = x * cw[CONV_W - 1:CONV_W, :]
    for s in range(1, CONV_W):
        shifted = jnp.where(r < s, pltpu.roll(prev, (s - seq_len) % CHUNK, axis=0),
                            pltpu.roll(x, s, axis=0))
        conv = conv + shifted * cw[CONV_W - 1 - s:CONV_W - s, :]
    c = conv * _sigmoid(conv)

    masks = _chunk_masks(seq_len)
    i, j, same, incl, strict = masks
    gates = gates_ref[...]
    gc_cols = _dot_f32(incl.astype(F32), gates)
    gc_rows = [_dot_f32(gt_ref[0], (same & (i <= j)).astype(F32))]
    gl_cols = _dot_f32(same.astype(F32), gates)
    rows = _div(lax.broadcasted_iota(jnp.int32, (2 * CHUNK, 1), 0) & (CHUNK - 1), seq_len)
    u, w, qk, q_dec, k_dec, gl = _gdn_pre(c, gates, gc_cols, gc_rows, gl_cols, masks, seq_len)
    lhs = jnp.concatenate([w, q_dec], axis=1)
    both = jnp.zeros((H_A, 2 * CHUNK, DV), F32)
    for q in range(nseq):
        both = jnp.where(rows == q, _bdot(lhs, s_in_ref[q]), both)
    v_new = u - both[:, :CHUNK]
    o = both[:, CHUNK:] + _bdot(qk, v_new)
    for q in range(nseq):
        g_tot = jnp.exp(gl[:, q * seq_len:q * seq_len + 1, :])
        s_out_ref[q] = s_in_ref[q] * g_tot + _bdot_tn(jnp.where(rows[:CHUNK] == q, k_dec, 0.0), v_new)
    _gdn_store(o_ref, o, on_ref[...], _heads(proj_ref[:, QKV_DIM:], DV))


def _gdn_sample(proj, gates, gates_t, conv_w, o_norm, prev8, states, layer, row0, bsz, t_len):
    nseq = CHUNK // t_len
    steps = bsz // nseq
    blk0 = row0 // CHUNK
    n_proj = proj.shape[1]
    row = lambda g: (blk0 + g, 0)
    return pl.pallas_call(
        functools.partial(_gdn_sample_body, seq_len=t_len),
        grid=(steps,),
        in_specs=[
            pl.BlockSpec((CHUNK, n_proj), row),
            pl.BlockSpec((CHUNK, LANES), row),
            pl.BlockSpec((1, 2 * H_A, CHUNK), lambda g: (blk0 + g, 0, 0)),
            pl.BlockSpec((CONV_W, QKV_DIM), lambda g: (0, 0)),
            pl.BlockSpec((1, DV), lambda g: (0, 0)),
            pl.BlockSpec((nseq, SUBLANES, QKV_DIM), lambda g: (g, 0, 0)),
            pl.BlockSpec((None, nseq, H_A, DK, DV), lambda g: (layer, g, 0, 0, 0)),
        ],
        out_specs=[
            pl.BlockSpec((CHUNK, H_A * DV), lambda g: (g, 0)),
            pl.BlockSpec((nseq, H_A, DK, DV), lambda g: (g, 0, 0, 0)),
        ],
        out_shape=[
            jax.ShapeDtypeStruct((bsz * t_len, H_A * DV), F32),
            jax.ShapeDtypeStruct((bsz, H_A, DK, DV), F32),
        ],
        compiler_params=_cparams(("parallel",)),
        name="gdn_sample",
    )(proj, gates, gates_t, conv_w, o_norm.reshape(1, DV), prev8, states)


LOG2E = 1.4426950408889634
SB_QSCALE = (HD_B ** -0.5) * LOG2E


def _softplus2(x):
    neg_abs = lax.bitcast_convert_type(
        lax.bitcast_convert_type(x, jnp.uint32) | jnp.uint32(0x80000000), F32)
    return jnp.maximum(x, 0.0) + jnp.log2(1.0 + jnp.exp2(neg_abs))


def _sb_mass(z, mask, u):
    sp = _softplus2(z)
    if mask is not None:
        sp = jnp.where(mask, sp, 0.0)
    return jnp.dot(sp.astype(BF16), u, preferred_element_type=F32)


def _sb_weights(z, mask, csum):
    att = jnp.exp2(z - csum)
    if mask is not None:
        att = jnp.where(mask, att, 0.0)
    return att


def _sb_prompt_body(qt_ref, kt_ref, bias_ref, q_ref, k_ref, v_ref, u_ref, o_ref, acc_ref, run_ref):
    hp = pl.program_id(1)
    p = pl.program_id(2)
    qi = qt_ref[p]
    kj = kt_ref[p]
    heads = SB_W // HD_B
    ratio = SB_BQ // SB_BK
    diag = kj >= qi * ratio

    @pl.when(kj == qi * ratio + (ratio - 1))
    def _():
        acc_ref[...] = jnp.zeros_like(acc_ref)
        run_ref[...] = jnp.zeros_like(run_ref)

    lane = _div(lax.broadcasted_iota(jnp.int32, (1, SB_W), 1), HD_B)
    vrow = _div(lax.broadcasted_iota(jnp.int32, (SB_W, 1), 0), HD_B)

    def step(masked, lo):
        q = q_ref[lo:, :] * SB_QSCALE
        k = k_ref[...].astype(BF16)
        v = v_ref[...]
        u = u_ref[...]
        mask = None
        if masked:
            qpos = lax.broadcasted_iota(jnp.int32, (SB_BQ - lo, SB_BK), 0) + (qi * SB_BQ + lo)
            kpos = lax.broadcasted_iota(jnp.int32, (SB_BQ - lo, SB_BK), 1) + kj * SB_BK
            mask = kpos < qpos
        acc = acc_ref[lo:, :]
        for a in range(heads):
            z = _dot(jnp.where(lane == a, q, 0.0), k) + bias_ref[hp * heads + a] * LOG2E
            csum = run_ref[a, lo:] + _sb_mass(z, mask, u)
            acc = acc + _dot_nt(_sb_weights(z, mask, csum), jnp.where(vrow == a, v, 0.0))
            run_ref[a, lo:] = csum[:, 0:1]
        acc_ref[lo:, :] = acc

    for c in range(ratio):
        @pl.when(kj == qi * ratio + c)
        def _():
            step(True, c * SB_BK)

    @pl.when(jnp.logical_not(diag))
    def _():
        step(False, 0)

    @pl.when(kj == 0)
    def _():
        o_ref[...] = acc_ref[...]


def _sb_prompt(q, k, v, bias, bsz, t_len):
    d = H_B * HD_B
    nq = t_len // SB_BQ
    ratio = SB_BQ // SB_BK
    pairs = [(qi, kj) for qi in range(nq) for kj in range(qi * ratio + ratio - 1, -1, -1)]
    qt = jnp.array([pq for pq, _ in pairs], jnp.int32)
    kt = jnp.array([pk for _, pk in pairs], jnp.int32)
    ii = lax.broadcasted_iota(jnp.int32, (SB_BK, SB_BK), 0)
    jj = lax.broadcasted_iota(jnp.int32, (SB_BK, SB_BK), 1)
    u = (ii >= jj).astype(BF16)
    heads = SB_W // HD_B
    grid_spec = pltpu.PrefetchScalarGridSpec(
        num_scalar_prefetch=2,
        grid=(bsz, d // SB_W, len(pairs)),
        in_specs=[
            pl.BlockSpec(memory_space=pltpu.SMEM),
            pl.BlockSpec((SB_BQ, SB_W), lambda b, h, p, qt, kt: (b * nq + qt[p], h)),
            pl.BlockSpec((None, SB_W, SB_BK), lambda b, h, p, qt, kt: (b, h, kt[p])),
            pl.BlockSpec((None, SB_W, SB_BK), lambda b, h, p, qt, kt: (b, h, kt[p])),
            pl.BlockSpec((SB_BK, SB_BK), lambda b, h, p, qt, kt: (0, 0)),
        ],
        out_specs=pl.BlockSpec((SB_BQ, SB_W), lambda b, h, p, qt, kt: (b * nq + qt[p], h)),
        scratch_shapes=[pltpu.VMEM((SB_BQ, SB_W), F32), pltpu.VMEM((heads, SB_BQ, 1), F32)],
    )
    return pl.pallas_call(
        _sb_prompt_body,
        grid_spec=grid_spec,
        out_shape=jax.ShapeDtypeStruct((bsz * t_len, d), F32),
        compiler_params=_cparams(("parallel", "parallel", "arbitrary")),
        name="sb_prompt",
    )(qt, kt, bias, q, k, v, u)


def _sb_sample_body(pt_ref, bias_ref, q_ref, kn_ref, vn_ref, *rest, t_len):
    kp_refs = rest[:SB_PAGES]
    vp_refs = rest[SB_PAGES:2 * SB_PAGES]
    u_ref, o_ref, qbd_ref, acc_ref, run_ref, bcol_ref = rest[2 * SB_PAGES:]
    s = pl.program_id(1)
    d = H_B * HD_B
    rows = H_B * t_len
    rowh = _div(lax.broadcasted_iota(jnp.int32, (rows, 1), 0), t_len)
    colh = _div(lax.broadcasted_iota(jnp.int32, (1, d), 1), HD_B)
    u = u_ref[...]

    def blocks(kbs, vbs, masks):
        zs = [(_dot_nt(qbd_ref[...], kb()) if m is not None else _dot(qbd_ref[...], kb())) + bcol_ref[...]
              for kb, m in zip(kbs, masks)]
        masses = [_sb_mass(z, m, u) for z, m in zip(zs, masks)]
        run = run_ref[...]
        pv = acc_ref[...]
        for z, m, mass, vb in zip(zs, masks, masses, vbs):
            csum = run + mass
            att = _sb_weights(z, m, csum)
            pv = pv + (_dot(att, vb()) if m is not None else _dot_nt(att, vb()))
            run = csum[:, 0:1]
        acc_ref[...] = pv
        run_ref[...] = run

    pages_k = [functools.partial(lambda r: r[0], r) for r in kp_refs]
    pages_v = [functools.partial(lambda r: r[0], r) for r in vp_refs]

    @pl.when(s == 0)
    def _():
        q = q_ref[0] * SB_QSCALE
        qbd_ref[...] = jnp.where(rowh == colh, jnp.concatenate([q] * H_B, axis=0), 0.0).astype(BF16)
        bcol = jnp.zeros((rows, 1), F32)
        for h in range(H_B):
            bcol = jnp.where(rowh == h, bias_ref[h] * LOG2E, bcol)
        bcol_ref[...] = bcol
        acc_ref[...] = jnp.zeros_like(acc_ref)
        run_ref[...] = jnp.zeros_like(run_ref)
        pad = jnp.zeros((PAGE - t_len, d), F32)
        t = lax.broadcasted_iota(jnp.int32, (rows, PAGE), 0) & (t_len - 1)
        jk = lax.broadcasted_iota(jnp.int32, (rows, PAGE), 1)
        blocks([lambda: jnp.concatenate([kn_ref[0], pad], axis=0)] + pages_k,
               [lambda: jnp.concatenate([vn_ref[0], pad], axis=0)] + pages_v,
               [jk < t] + [None] * SB_PAGES)

    @pl.when(s != 0)
    def _():
        blocks(pages_k, pages_v, [None] * SB_PAGES)

    @pl.when(s == pl.num_programs(1) - 1)
    def _():
        g = jnp.where(rowh == colh, acc_ref[...], 0.0)
        o_ref[0] = jnp.sum(g.reshape(H_B, t_len, d), axis=0)


def _sb_sample(q, k_new, v_new, cache_k, cache_v, page_table, bias, bsz, t_len):
    d = H_B * HD_B
    n_pages = page_table.shape[1]
    rows = H_B * t_len
    ii = lax.broadcasted_iota(jnp.int32, (PAGE, PAGE), 0)
    jj = lax.broadcasted_iota(jnp.int32, (PAGE, PAGE), 1)
    u = (ii >= jj).astype(BF16)
    tok = lambda b, s, pt: (b, 0, 0)

    def page(p):
        return lambda b, s, pt: (pt[b * n_pages + (n_pages - 1 - (s * SB_PAGES + p))], 0, 0)

    page_specs = [pl.BlockSpec((1, d, PAGE), page(p)) for p in range(SB_PAGES)]
    grid_spec = pltpu.PrefetchScalarGridSpec(
        num_scalar_prefetch=1,
        grid=(bsz, n_pages // SB_PAGES),
        in_specs=[
            pl.BlockSpec(memory_space=pltpu.SMEM),
            pl.BlockSpec((1, t_len, d), tok),
            pl.BlockSpec((1, t_len, d), tok),
            pl.BlockSpec((1, t_len, d), tok),
            *page_specs,
            *page_specs,
            pl.BlockSpec((PAGE, PAGE), lambda b, s, pt: (0, 0)),
        ],
        out_specs=pl.BlockSpec((1, t_len, d), tok),
        scratch_shapes=[
            pltpu.VMEM((rows, d), BF16),
            pltpu.VMEM((rows, d), F32),
            pltpu.VMEM((rows, 1), F32),
            pltpu.VMEM((rows, 1), F32),
        ],
    )
    out = pl.pallas_call(
        functools.partial(_sb_sample_body, t_len=t_len),
        grid_spec=grid_spec,
        out_shape=jax.ShapeDtypeStruct((bsz, t_len, d), F32),
        compiler_params=_cparams(("parallel", "arbitrary")),
        name="sb_sample",
    )(page_table.reshape(-1), bias, q.reshape(bsz, t_len, d), k_new.reshape(bsz, t_len, d),
      v_new.reshape(bsz, t_len, d), *([cache_k] * SB_PAGES), *([cache_v] * SB_PAGES), u)
    return out.reshape(bsz * t_len, d)


def kernel(x_prompt, x_sample, cache_k, cache_v, state_conv, state_rec, page_table, p_prompt, p_sample, ffn1_norm, ffn1_w_in, ffn1_w_out, mix_norm, ffn2_norm, ffn2_w_in, ffn2_w_out, ple_w_in, ple_norm, ple_w_gate, a_w_in, a_conv, a_A_log, a_dt_bias, a_o_norm, a_w_out, kv_norm, w_kv, b_w_q, b_bias, b_w_o, final_norm):
    bp, tp, d = x_prompt.shape
    bs, ts, _ = x_sample.shape
    depth = ffn1_norm.shape[0]
    n_a = a_w_in.shape[0]
    mp = bp * tp
    ms = bs * ts
    dkv = H_B * HD_B
    n_main = QKV_DIM + H_A * DV

    x = jnp.concatenate([x_prompt.reshape(mp, d), x_sample.reshape(ms, d)], axis=0)
    p_all = jnp.concatenate([p_prompt.reshape(depth, mp, -1), p_sample.reshape(depth, ms, -1)], axis=1)
    ck = cache_k.transpose(0, 2, 3, 1).reshape(cache_k.shape[0], dkv, cache_k.shape[1])
    cv = cache_v.transpose(0, 2, 3, 1).reshape(cache_v.shape[0], dkv, cache_v.shape[1])
    prev8 = jnp.pad(state_conv, ((0, 0), (0, 0), (SUBLANES - (CONV_W - 1), 0), (0, 0)))

    convs_p, convs_s, recs_p, recs_s = [], [], [], []
    for i in range(depth):
        x = _ffn(x, ffn1_norm[i], ffn1_w_in[i].astype(BF16), ffn1_w_out[i].astype(BF16))
        if i < n_a:
            proj, gates, gates_t = _gdn_proj(
                x, mix_norm[i], a_w_in[i][:, :n_main].astype(BF16), a_w_in[i][:, n_main:],
                a_A_log[i], a_dt_bias[i])
            gt = gates_t.reshape(2 * H_A, -1, CHUNK).transpose(1, 0, 2)
            o_p, s_p = _gdn_prompt(proj, gates, gt, a_conv[i], a_o_norm[i], bp, tp)
            o_s, s_s = _gdn_sample(proj, gates, gt, a_conv[i], a_o_norm[i], prev8[i], state_rec, i,
                                   mp, bs, ts)
            convs_p.append(jnp.stack([proj[(b + 1) * tp - (CONV_W - 1):(b + 1) * tp, :QKV_DIM]
                                      for b in range(bp)]))
            convs_s.append(proj[mp:, :QKV_DIM].reshape(bs, ts, QKV_DIM)[:, ts - (CONV_W - 1):])
            recs_p.append(s_p)
            recs_s.append(s_s)
            x = _matmul_residual(x, o_p, o_s, a_w_out[i].astype(BF16))
        else:
            ib = i - n_a
            q = _norm_matmul(x, mix_norm[i], b_w_q[ib].astype(BF16))
            o_p = _sb_prompt(q, kt_p, vt_p, b_bias[ib], bp, tp)
            o_s = _sb_sample(q[mp:], k_s, v_s, ck, cv, page_table, b_bias[ib], bs, ts)
            x = _matmul_residual(x, o_p, o_s, b_w_o[ib].astype(BF16))
        x = _ffn(x, ffn2_norm[i], ffn2_w_in[i].astype(BF16), ffn2_w_out[i].astype(BF16))
        ple_args = (x, p_all[i], ple_norm[i], ple_w_in[i].astype(BF16), ple_w_gate[i].astype(BF16))
        if i < depth - 1:
            x = _ple(*ple_args)
        else:
            y_p, y_s = _ple_final(*ple_args, final_norm, mp)
        if i == n_a - 1:
            wk = w_kv[:, :dkv].astype(BF16)
            wv = w_kv[:, dkv:].astype(BF16)
            kt_p, vt_p = _norm_matmul_t(x, kv_norm, wk.T, wv.T, bp, tp)
            k_s = _norm_matmul(x[mp:], kv_norm, wk)
            v_s = _norm_matmul(x[mp:], kv_norm, wv)

    y_prompt = y_p.reshape(bp, tp, d)
    y_sample = y_s.reshape(bs, ts, d)
    k_prompt = kt_p.reshape(bp, H_B, HD_B, tp).transpose(0, 3, 1, 2)
    v_prompt = vt_p.reshape(bp, H_B, HD_B, tp).transpose(0, 3, 1, 2)
    k_sample = k_s.reshape(bs, ts, H_B, HD_B)
    v_sample = v_s.reshape(bs, ts, H_B, HD_B)
    return (y_prompt, y_sample, jnp.stack(convs_p), jnp.stack(recs_p), k_prompt, v_prompt,
            jnp.stack(convs_s), jnp.stack(recs_s), k_sample, v_sample)
```

```python
import functools

import jax
import jax.numpy as jnp
from jax import lax
from jax.experimental import pallas as pl
from jax.experimental.pallas import tpu as pltpu

F32 = jnp.float32
BF16 = jnp.bfloat16
EPS = 1e-6

H_A = 8
DK = 128
DV = 128
QKV_DIM = H_A * (2 * DK + DV)
CONV_W = 4
CHUNK = 64
GDN_CHUNKS = 4
H_B = 16
HD_B = 64
PAGE = 128

LANES = 128
SUBLANES = 8
VMEM_LIMIT = 48 * 1024 * 1024

ROW_TILE = 1024
FF_TILE = 256
SB_BQ = 1024
SB_BK = 256
SB_W = 256
SB_PAGES = 8


def _cparams(sem):
    return pltpu.CompilerParams(dimension_semantics=sem, vmem_limit_bytes=VMEM_LIMIT)


def _rms(x, g):
    ms = jnp.mean(x * x, axis=-1, keepdims=True)
    return x * lax.rsqrt(ms + EPS) * g


def _sigmoid(x):
    return 1.0 / (1.0 + jnp.exp(-x))


def _softplus(x):
    return jnp.maximum(x, 0.0) + jnp.log1p(jnp.exp(-jnp.abs(x)))


def _dot(a, b):
    return jnp.dot(a.astype(BF16), b.astype(BF16), preferred_element_type=F32)


def _dot_nt(a, b):
    return lax.dot_general(a.astype(BF16), b.astype(BF16), (((1,), (1,)), ((), ())),
                           preferred_element_type=F32)


def _dot_f32(a, b):
    return jnp.dot(a, b, preferred_element_type=F32, precision=lax.Precision.HIGHEST)


def _ffn_body(x_ref, g_ref, wg_ref, wu_ref, wo_ref, o_ref, h_ref):
    @pl.when(pl.program_id(1) == 0)
    def _():
        x = x_ref[...]
        h_ref[...] = _rms(x, g_ref[...]).astype(BF16)
        o_ref[...] = x

    h = h_ref[...]
    gate = jnp.dot(h, wg_ref[...], preferred_element_type=F32)
    up = jnp.dot(h, wu_ref[...], preferred_element_type=F32)
    act = gate * _sigmoid(gate) * up
    o_ref[...] += 0.5 * _dot(act, wo_ref[...])


def _ffn(x, g, w_in, w_out):
    m, d = x.shape
    f = w_out.shape[0]
    nf = f // FF_TILE
    return pl.pallas_call(
        _ffn_body,
        grid=(m // ROW_TILE, nf),
        in_specs=[
            pl.BlockSpec((ROW_TILE, d), lambda i, j: (i, 0)),
            pl.BlockSpec((1, d), lambda i, j: (0, 0)),
            pl.BlockSpec((d, FF_TILE), lambda i, j: (0, j)),
            pl.BlockSpec((d, FF_TILE), lambda i, j: (0, j + nf)),
            pl.BlockSpec((FF_TILE, d), lambda i, j: (j, 0)),
        ],
        out_specs=pl.BlockSpec((ROW_TILE, d), lambda i, j: (i, 0)),
        out_shape=jax.ShapeDtypeStruct((m, d), F32),
        scratch_shapes=[pltpu.VMEM((ROW_TILE, d), BF16)],
        compiler_params=_cparams(("parallel", "arbitrary")),
        name="ffn",
    )(x, g.reshape(1, d), w_in, w_in, w_out)


FUSED_VMEM_LIMIT = 58 * 1024 * 1024


def _ffn_tail_body(x_ref, ya_ref, yb_ref, wm_ref, g_ref, wg_ref, wu_ref, wo_ref, *rest, nf, tiles_a, ple):
    if ple:
        p_ref, gp_ref, wpe_ref, wpg_ref, o_ref, h_ref = rest
    else:
        o_ref, h_ref = rest
    i = pl.program_id(0)
    j = pl.program_id(1)

    def start(y_ref):
        x = x_ref[...] + _dot(y_ref[...], wm_ref[...])
        h_ref[...] = _rms(x, g_ref[...]).astype(BF16)
        o_ref[...] = x

    @pl.when((j == 0) & (i < tiles_a))
    def _():
        start(ya_ref)

    @pl.when((j == 0) & (i >= tiles_a))
    def _():
        start(yb_ref)

    h = h_ref[...]
    gate = jnp.dot(h, wg_ref[...], preferred_element_type=F32)
    up = jnp.dot(h, wu_ref[...], preferred_element_type=F32)
    act = gate * _sigmoid(gate) * up
    o_ref[...] += 0.5 * _dot(act, wo_ref[...])

    if ple:
        @pl.when(j == nf - 1)
        def _():
            x = o_ref[...]
            pgate = _sigmoid(_dot(_rms(x, gp_ref[...]), wpg_ref[...]))
            o_ref[...] = x + _dot(p_ref[...], wpe_ref[...]) * pgate


def _ffn_tail(x, ya, yb, w_mix, g, w_in, w_out, ple=None):
    m, d = x.shape
    f = w_out.shape[0]
    nf = f // FF_TILE
    k = ya.shape[1]
    tiles_a = ya.shape[0] // ROW_TILE
    assert ya.shape[0] % ROW_TILE == 0 and yb.shape[0] % ROW_TILE == 0
    row = lambda i, j: (i, 0)
    fixed = lambda i, j: (0, 0)
    once = dict(pipeline_mode=pl.Buffered(1))
    in_specs = [
        pl.BlockSpec((ROW_TILE, d), row),
        pl.BlockSpec((ROW_TILE, k), lambda i, j: (jnp.minimum(i, tiles_a - 1), 0)),
        pl.BlockSpec((ROW_TILE, k), lambda i, j: (jnp.maximum(i - tiles_a, 0), 0)),
        pl.BlockSpec((k, d), fixed, **once),
        pl.BlockSpec((1, d), fixed),
        pl.BlockSpec((d, FF_TILE), lambda i, j: (0, j)),
        pl.BlockSpec((d, FF_TILE), lambda i, j: (0, j + nf)),
        pl.BlockSpec((FF_TILE, d), lambda i, j: (j, 0)),
    ]
    args = [x, ya, yb, w_mix, g.reshape(1, d), w_in, w_in, w_out]
    if ple is not None:
        p, g_ple, w_pe, w_pg = ple
        pd = p.shape[1]
        in_specs += [
            pl.BlockSpec((ROW_TILE, pd), row),
            pl.BlockSpec((1, d), fixed),
            pl.BlockSpec((pd, d), fixed, **once),
            pl.BlockSpec((d, d), fixed, **once),
        ]
        args += [p, g_ple.reshape(1, d), w_pe, w_pg]
    return pl.pallas_call(
        functools.partial(_ffn_tail_body, nf=nf, tiles_a=tiles_a, ple=ple is not None),
        grid=(m // ROW_TILE, nf),
        in_specs=in_specs,
        out_specs=pl.BlockSpec((ROW_TILE, d), row),
        out_shape=jax.ShapeDtypeStruct((m, d), F32),
        scratch_shapes=[pltpu.VMEM((ROW_TILE, d), BF16)],
        compiler_params=pltpu.CompilerParams(dimension_semantics=("arbitrary", "arbitrary"),
                                             vmem_limit_bytes=FUSED_VMEM_LIMIT),
        name="ffn_tail",
    )(*args)


def _nmm_body(x_ref, g_ref, w_ref, o_ref, h_ref):
    @pl.when(pl.program_id(1) == 0)
    def _():
        h_ref[...] = _rms(x_ref[...], g_ref[...]).astype(BF16)

    o_ref[...] = jnp.dot(h_ref[...], w_ref[...], preferred_element_type=F32)


def _norm_matmul(x, g, w, tn=1024):
    m, d = x.shape
    n = w.shape[1]
    return pl.pallas_call(
        _nmm_body,
        grid=(m // ROW_TILE, n // tn),
        in_specs=[
            pl.BlockSpec((ROW_TILE, d), lambda i, j: (i, 0)),
            pl.BlockSpec((1, d), lambda i, j: (0, 0)),
            pl.BlockSpec((d, tn), lambda i, j: (0, j)),
        ],
        out_specs=pl.BlockSpec((ROW_TILE, tn), lambda i, j: (i, j)),
        out_shape=jax.ShapeDtypeStruct((m, n), F32),
        scratch_shapes=[pltpu.VMEM((ROW_TILE, d), BF16)],
        compiler_params=_cparams(("parallel", "arbitrary")),
        name="norm_matmul",
    )(x, g.reshape(1, d), w)


def _nmm_t_body(x_ref, g_ref, wa_ref, wb_ref, oa_ref, ob_ref):
    h = _rms(x_ref[...], g_ref[...]).astype(BF16)
    oa_ref[...] = _dot_nt(wa_ref[...], h)
    ob_ref[...] = _dot_nt(wb_ref[...], h)


def _norm_matmul_t(x, g, wa_t, wb_t, bsz, t_len):
    d = x.shape[1]
    n = wa_t.shape[0]
    nt = t_len // ROW_TILE
    out = pl.BlockSpec((None, n, ROW_TILE), lambda b, i: (b, 0, i))
    return pl.pallas_call(
        _nmm_t_body,
        grid=(bsz, nt),
        in_specs=[
            pl.BlockSpec((ROW_TILE, d), lambda b, i: (b * nt + i, 0)),
            pl.BlockSpec((1, d), lambda b, i: (0, 0)),
            pl.BlockSpec((n, d), lambda b, i: (0, 0)),
            pl.BlockSpec((n, d), lambda b, i: (0, 0)),
        ],
        out_specs=[out, out],
        out_shape=[jax.ShapeDtypeStruct((bsz, n, t_len), F32)] * 2,
        compiler_params=_cparams(("parallel", "parallel")),
        name="norm_matmul_t",
    )(x, g.reshape(1, d), wa_t, wb_t)


def _gdn_proj_body(x_ref, g_ref, w_ref, wab_ref, wabt_ref, prow_ref, pcol_ref,
                   o_ref, gates_ref, gatest_ref, h_ref):
    @pl.when(pl.program_id(1) == 0)
    def _():
        h = _rms(x_ref[...], g_ref[...]).astype(BF16)
        h_ref[...] = h
        ab = jnp.dot(h, wab_ref[...], preferred_element_type=F32)
        lane = lax.broadcasted_iota(jnp.int32, (1, LANES), 1)
        gv = -jnp.exp(prow_ref[0:1, :]) * _softplus(ab + prow_ref[1:2, :])
        gates_ref[...] = jnp.where(lane < H_A, gv, jnp.where(lane < 2 * H_A, _sigmoid(ab), 0.0))
        abt = lax.dot_general(wabt_ref[...], h, (((1,), (1,)), ((), ())),
                              preferred_element_type=F32)
        row = lax.broadcasted_iota(jnp.int32, (2 * H_A, 1), 0)
        gvt = -jnp.exp(pcol_ref[:, 0:1]) * _softplus(abt + pcol_ref[:, 1:2])
        gatest_ref[...] = jnp.where(row < H_A, gvt, _sigmoid(abt))

    o_ref[...] = jnp.dot(h_ref[...], w_ref[...], preferred_element_type=F32)


def _gdn_proj(x, g, w_main, w_ab, a_log, dt_bias, tn=2048):
    m, d = x.shape
    n = w_main.shape[1]
    wab = jnp.pad(w_ab, ((0, 0), (0, LANES - 2 * H_A))).astype(BF16)
    wabt = w_ab.T.astype(BF16)
    zeros = jnp.zeros((H_A,), F32)
    prow = jnp.stack([jnp.pad(a_log, (0, LANES - H_A)), jnp.pad(dt_bias, (0, LANES - H_A))])
    prow = jnp.pad(prow, ((0, SUBLANES - 2), (0, 0)))
    pcol = jnp.stack([jnp.concatenate([a_log, zeros]), jnp.concatenate([dt_bias, zeros])], axis=1)
    pcol = jnp.pad(pcol, ((0, 0), (0, LANES - 2)))
    return pl.pallas_call(
        _gdn_proj_body,
        grid=(m // ROW_TILE, n // tn),
        in_specs=[
            pl.BlockSpec((ROW_TILE, d), lambda i, j: (i, 0)),
            pl.BlockSpec((1, d), lambda i, j: (0, 0)),
            pl.BlockSpec((d, tn), lambda i, j: (0, j)),
            pl.BlockSpec((d, LANES), lambda i, j: (0, 0)),
            pl.BlockSpec((2 * H_A, d), lambda i, j: (0, 0)),
            pl.BlockSpec((SUBLANES, LANES), lambda i, j: (0, 0)),
            pl.BlockSpec((2 * H_A, LANES), lambda i, j: (0, 0)),
        ],
        out_specs=[
            pl.BlockSpec((ROW_TILE, tn), lambda i, j: (i, j)),
            pl.BlockSpec((ROW_TILE, LANES), lambda i, j: (i, 0)),
            pl.BlockSpec((2 * H_A, ROW_TILE), lambda i, j: (0, i)),
        ],
        out_shape=[
            jax.ShapeDtypeStruct((m, n), F32),
            jax.ShapeDtypeStruct((m, LANES), F32),
            jax.ShapeDtypeStruct((2 * H_A, m), F32),
        ],
        scratch_shapes=[pltpu.VMEM((ROW_TILE, d), BF16)],
        compiler_params=_cparams(("parallel", "arbitrary")),
        name="gdn_proj",
    )(x, g.reshape(1, d), w_main, wab, wabt, prow, pcol)


def _ple_final_body(x_ref, p_ref, g_ref, wpe_ref, wpg_ref, gf_ref, oa_ref, ob_ref, *, tiles_a):
    x = x_ref[...]
    gate = _sigmoid(_dot(_rms(x, g_ref[...]), wpg_ref[...]))
    y = _rms(x + _dot(p_ref[...], wpe_ref[...]) * gate, gf_ref[...])
    i = pl.program_id(0)

    @pl.when(i < tiles_a)
    def _():
        oa_ref[...] = y

    @pl.when(i >= tiles_a)
    def _():
        ob_ref[...] = y


def _ple_final(x, p, g, w_pe, w_pg, g_final, rows_a):
    m, d = x.shape
    pd = p.shape[1]
    tiles_a = rows_a // ROW_TILE
    assert rows_a % ROW_TILE == 0 and (m - rows_a) % ROW_TILE == 0
    row = lambda i: (i, 0)
    fixed = lambda i: (0, 0)
    return pl.pallas_call(
        functools.partial(_ple_final_body, tiles_a=tiles_a),
        grid=(m // ROW_TILE,),
        in_specs=[
            pl.BlockSpec((ROW_TILE, d), row),
            pl.BlockSpec((ROW_TILE, pd), row),
            pl.BlockSpec((1, d), fixed),
            pl.BlockSpec((pd, d), fixed),
            pl.BlockSpec((d, d), fixed),
            pl.BlockSpec((1, d), fixed),
        ],
        out_specs=[
            pl.BlockSpec((ROW_TILE, d), lambda i: (jnp.minimum(i, tiles_a - 1), 0)),
            pl.BlockSpec((ROW_TILE, d), lambda i: (jnp.maximum(i - tiles_a, 0), 0)),
        ],
        out_shape=[
            jax.ShapeDtypeStruct((rows_a, d), F32),
            jax.ShapeDtypeStruct((m - rows_a, d), F32),
        ],
        compiler_params=_cparams(("arbitrary",)),
        name="ple_final",
    )(x, p, g.reshape(1, d), w_pe, w_pg, g_final.reshape(1, d))


def _div(x, n):
    assert n & (n - 1) == 0
    return lax.shift_right_logical(x, n.bit_length() - 1)


def _chunk_masks(seq_len):
    i = lax.broadcasted_iota(jnp.int32, (CHUNK, CHUNK), 0)
    j = lax.broadcasted_iota(jnp.int32, (CHUNK, CHUNK), 1)
    same = _div(i, seq_len) == _div(j, seq_len)
    incl = same & (i >= j)
    strict = same & (i > j)
    return i, j, same, incl, strict


def _unit_lower_inverse(a, i, j, seq_len):
    base = SUBLANES
    eye = (i == j).astype(F32)
    dg = jnp.where(_div(i, base) == _div(j, base), a, 0.0)
    p1 = _bdot(dg, dg)
    p2 = _bdot(p1, p1)
    y = eye - dg
    y = y + _bdot(y, p1)
    x = y + _bdot(y, p2)
    b = base
    while b < seq_len:
        cm = jnp.where((_div(i, 2 * b) == _div(j, 2 * b)) & (_div(i, b) != _div(j, b)), a, 0.0)
        x = x - _bdot(_bdot(x, cm), x)
        b *= 2
    return x


def _bdot(a, b):
    return lax.dot_general(a.astype(BF16), b.astype(BF16), (((2,), (1,)), ((0,), (0,))),
                           preferred_element_type=F32)


def _bdot_nt(a, b):
    return lax.dot_general(a.astype(BF16), b.astype(BF16), (((2,), (2,)), ((0,), (0,))),
                           preferred_element_type=F32)


def _bdot_tn(a, b):
    return lax.dot_general(a.astype(BF16), b.astype(BF16), (((1,), (1,)), ((0,), (0,))),
                           preferred_element_type=F32)


def _split(a):
    hi = a.astype(BF16)
    return hi, (a - hi.astype(F32)).astype(BF16)


def _bdot3(a, b):
    ah, al = _split(a)
    bh, bl = _split(b)
    return _bdot(ah, bh) + (_bdot(ah, bl) + _bdot(al, bh))


def _heads(x, width, first=0):
    return jnp.stack([x[n * CHUNK:(n + 1) * CHUNK, first + h * width:first + (h + 1) * width]
                      for n in range(x.shape[0] // CHUNK) for h in range(H_A)])


def _gdn_pre(c, gates, gc_cols, gc_rows, gl_cols, masks, seq_len):
    i, j, same, incl, strict = masks
    q = _heads(c, DK)
    k = _heads(c, DK, H_A * DK)
    v = _heads(c, DV, 2 * H_A * DK)
    q = q * lax.rsqrt(jnp.sum(q * q, axis=-1, keepdims=True) + EPS) * (DK ** -0.5)
    k = k * lax.rsqrt(jnp.sum(k * k, axis=-1, keepdims=True) + EPS)
    gcc = _heads(gc_cols, 1)
    gl = _heads(gl_cols, 1)
    beta = _heads(gates, 1, H_A)
    gcr = jnp.stack([g[h:h + 1, :] for g in gc_rows for h in range(H_A)])
    decay = jnp.where(incl, jnp.exp(jnp.where(incl, gcc - gcr, 0.0)), 0.0)
    kb = k * beta
    a = jnp.where(strict, _bdot_nt(kb, k) * decay, 0.0)
    qk = _bdot_nt(q, k) * decay
    x = _unit_lower_inverse(a, i, j, seq_len)
    eg = jnp.exp(gcc)
    rhs = jnp.concatenate([v * beta, kb * eg], axis=2)
    sol0 = _bdot(x, rhs)
    sol = sol0 + _bdot(x, rhs - sol0 - _bdot3(a, sol0))
    u = sol[:, :, :DV]
    w = sol[:, :, DV:]
    q_dec = q * eg
    k_dec = k * jnp.exp(gl - gcc)
    return u, w, qk, q_dec, k_dec, gl


def _gdn_store(o_ref, o, onorm, gate):
    on = o * lax.rsqrt(jnp.mean(o * o, axis=-1, keepdims=True) + EPS) * onorm
    out = on * (gate * _sigmoid(gate))
    for n in range(out.shape[0] // H_A):
        for h in range(H_A):
            o_ref[n * CHUNK:(n + 1) * CHUNK, h * DV:(h + 1) * DV] = out[n * H_A + h]


def _gdn_prompt_body(proj_ref, gates_ref, gt_ref, cw_ref, on_ref, o_ref, s_out_ref, s_ref, tail_ref):
    n = pl.program_id(1)

    @pl.when(n == 0)
    def _():
        s_ref[...] = jnp.zeros_like(s_ref)
        tail_ref[0:SUBLANES, :] = jnp.zeros((SUBLANES, QKV_DIM), F32)

    rows = GDN_CHUNKS * CHUNK
    x = proj_ref[:, :QKV_DIM]
    tail_ref[SUBLANES:, :] = x
    cw = cw_ref[...]
    conv = x * cw[CONV_W - 1:CONV_W, :]
    for s in range(1, CONV_W):
        conv = conv + tail_ref[SUBLANES - s:SUBLANES - s + rows, :] * cw[CONV_W - 1 - s:CONV_W - s, :]
    tail_ref[0:SUBLANES, :] = x[rows - SUBLANES:, :]
    c = conv * _sigmoid(conv)

    masks = _chunk_masks(CHUNK)
    i, j, same, incl, strict = masks
    gates = gates_ref[...]
    lower = incl.astype(F32)
    upper = (i <= j).astype(F32)
    gc_cols = jnp.concatenate([_dot_f32(lower, gates[m * CHUNK:(m + 1) * CHUNK])
                               for m in range(GDN_CHUNKS)], axis=0)
    gc_rows = [_dot_f32(gt_ref[m], upper) for m in range(GDN_CHUNKS)]
    gl_cols = jnp.concatenate(
        [jnp.broadcast_to(gc_cols[(m + 1) * CHUNK - 1:(m + 1) * CHUNK, :], (CHUNK, LANES))
         for m in range(GDN_CHUNKS)], axis=0)
    u, w, qk, q_dec, k_dec, gl = _gdn_pre(c, gates, gc_cols, gc_rows, gl_cols, masks, CHUNK)
    s = s_ref[...]
    outs = []
    for m in range(GDN_CHUNKS):
        hs = slice(m * H_A, (m + 1) * H_A)
        both = _bdot(jnp.concatenate([w[hs], q_dec[hs]], axis=1), s)
        v_new = u[hs] - both[:, :CHUNK]
        outs.append(both[:, CHUNK:] + _bdot(qk[hs], v_new))
        s = s * jnp.exp(gl[hs][:, 0:1, :]) + _bdot_tn(k_dec[hs], v_new)
    s_ref[...] = s
    _gdn_store(o_ref, jnp.concatenate(outs, axis=0), on_ref[...], _heads(proj_ref[:, QKV_DIM:], DV))

    @pl.when(n == pl.num_programs(1) - 1)
    def _():
        s_out_ref[0] = s_ref[...]


def _gdn_prompt(proj, gates, gates_t, conv_w, o_norm, bsz, t_len):
    rows = GDN_CHUNKS * CHUNK
    nc = t_len // rows
    n_proj = proj.shape[1]
    row = lambda b, n: (b * nc + n, 0)
    return pl.pallas_call(
        _gdn_prompt_body,
        grid=(bsz, nc),
        in_specs=[
            pl.BlockSpec((rows, n_proj), row),
            pl.BlockSpec((rows, LANES), row),
            pl.BlockSpec((GDN_CHUNKS, 2 * H_A, CHUNK), lambda b, n: (b * nc + n, 0, 0)),
            pl.BlockSpec((CONV_W, QKV_DIM), lambda b, n: (0, 0)),
            pl.BlockSpec((1, DV), lambda b, n: (0, 0)),
        ],
        out_specs=[
            pl.BlockSpec((rows, H_A * DV), row),
            pl.BlockSpec((1, H_A, DK, DV), lambda b, n: (b, 0, 0, 0)),
        ],
        out_shape=[
            jax.ShapeDtypeStruct((bsz * t_len, H_A * DV), F32),
            jax.ShapeDtypeStruct((bsz, H_A, DK, DV), F32),
        ],
        scratch_shapes=[pltpu.VMEM((H_A, DK, DV), F32), pltpu.VMEM((SUBLANES + rows, QKV_DIM), F32)],
        compiler_params=_cparams(("parallel", "arbitrary")),
        name="gdn_prompt",
    )(proj, gates, gates_t, conv_w, o_norm.reshape(1, DV))


def _gdn_sample_body(proj_ref, gates_ref, gt_ref, cw_ref, on_ref, prev_ref, s_in_ref,
                     o_ref, s_out_ref, *, seq_len):
    nseq = CHUNK // seq_len
    x = proj_ref[:, :QKV_DIM]
    prev = prev_ref[...].reshape(CHUNK, QKV_DIM)
    cw = cw_ref[...]
    r = lax.broadcasted_iota(jnp.int32, (CHUNK, 1), 0) & (seq_len - 1)
    conv = x * cw[CONV_W - 1:CONV_W, :]
    for s in range(1, CONV_W):
        shifted = jnp.where(r < s, pltpu.roll(prev, (s - seq_len) % CHUNK, axis=0),
                            pltpu.roll(x, s, axis=0))
        conv = conv + shifted * cw[CONV_W - 1 - s:CONV_W - s, :]
    c = conv * _sigmoid(conv)

    masks = _chunk_masks(seq_len)
    i, j, same, incl, strict = masks
    gates = gates_ref[...]
    gc_cols = _dot_f32(incl.astype(F32), gates)
    gc_rows = [_dot_f32(gt_ref[0], (same & (i <= j)).astype(F32))]
    gl_cols = _dot_f32(same.astype(F32), gates)
    rows = _div(lax.broadcasted_iota(jnp.int32, (2 * CHUNK, 1), 0) & (CHUNK - 1), seq_len)
    u, w, qk, q_dec, k_dec, gl = _gdn_pre(c, gates, gc_cols, gc_rows, gl_cols, masks, seq_len)
    lhs = jnp.concatenate([w, q_dec], axis=1)
    both = jnp.zeros((H_A, 2 * CHUNK, DV), F32)
    for q in range(nseq):
        both = jnp.where(rows == q, _bdot(lhs, s_in_ref[q]), both)
    v_new = u - both[:, :CHUNK]
    o = both[:, CHUNK:] + _bdot(qk, v_new)
    for q in range(nseq):
        g_tot = jnp.exp(gl[:, q * seq_len:q * seq_len + 1, :])
        s_out_ref[q] = s_in_ref[q] * g_tot + _bdot_tn(jnp.where(rows[:CHUNK] == q, k_dec, 0.0), v_new)
    _gdn_store(o_ref, o, on_ref[...], _heads(proj_ref[:, QKV_DIM:], DV))


def _gdn_sample(proj, gates, gates_t, conv_w, o_norm, prev8, states, layer, row0, bsz, t_len):
    nseq = CHUNK // t_len
    steps = bsz // nseq
    blk0 = row0 // CHUNK
    n_proj = proj.shape[1]
    row = lambda g: (blk0 + g, 0)
    return pl.pallas_call(
        functools.partial(_gdn_sample_body, seq_len=t_len),
        grid=(steps,),
        in_specs=[
            pl.BlockSpec((CHUNK, n_proj), row),
            pl.BlockSpec((CHUNK, LANES), row),
            pl.BlockSpec((1, 2 * H_A, CHUNK), lambda g: (blk0 + g, 0, 0)),
            pl.BlockSpec((CONV_W, QKV_DIM), lambda g: (0, 0)),
            pl.BlockSpec((1, DV), lambda g: (0, 0)),
            pl.BlockSpec((nseq, SUBLANES, QKV_DIM), lambda g: (g, 0, 0)),
            pl.BlockSpec((None, nseq, H_A, DK, DV), lambda g: (layer, g, 0, 0, 0)),
        ],
        out_specs=[
            pl.BlockSpec((CHUNK, H_A * DV), lambda g: (g, 0)),
            pl.BlockSpec((nseq, H_A, DK, DV), lambda g: (g, 0, 0, 0)),
        ],
        out_shape=[
            jax.ShapeDtypeStruct((bsz * t_len, H_A * DV), F32),
            jax.ShapeDtypeStruct((bsz, H_A, DK, DV), F32),
        ],
        compiler_params=_cparams(("parallel",)),
        name="gdn_sample",
    )(proj, gates, gates_t, conv_w, o_norm.reshape(1, DV), prev8, states)


LOG2E = 1.4426950408889634
SB_QSCALE = (HD_B ** -0.5) * LOG2E


def _softplus2(x):
    neg_abs = -jnp.abs(x)
    return jnp.maximum(x, 0.0) + jnp.log2(1.0 + jnp.exp2(neg_abs))


def _sb_mass(z, mask, u):
    sp = _softplus2(z)
    if mask is not None:
        sp = jnp.where(mask, sp, 0.0)
    return jnp.dot(sp.astype(BF16), u, preferred_element_type=F32)


def _sb_weights(z, mask, csum):
    att = jnp.exp2(z - csum)
    if mask is not None:
        att = jnp.where(mask, att, 0.0)
    return att


def _sb_prompt_body(qt_ref, kt_ref, bias_ref, q_ref, k_ref, v_ref, u_ref, o_ref, acc_ref, run_ref):
    hp = pl.program_id(1)
    p = pl.program_id(2)
    qi = qt_ref[p]
    kj = kt_ref[p]
    heads = SB_W // HD_B
    ratio = SB_BQ // SB_BK
    diag = kj >= qi * ratio

    @pl.when(kj == qi * ratio + (ratio - 1))
    def _():
        acc_ref[...] = jnp.zeros_like(acc_ref)
        run_ref[...] = jnp.zeros_like(run_ref)

    lane = _div(lax.broadcasted_iota(jnp.int32, (1, SB_W), 1), HD_B)
    vrow = _div(lax.broadcasted_iota(jnp.int32, (SB_W, 1), 0), HD_B)

    def step(masked, lo):
        q = q_ref[lo:, :] * SB_QSCALE
        k = k_ref[...].astype(BF16)
        v = v_ref[...]
        u = u_ref[...]
        mask = None
        if masked:
            qpos = lax.broadcasted_iota(jnp.int32, (SB_BQ - lo, SB_BK), 0) + (qi * SB_BQ + lo)
            kpos = lax.broadcasted_iota(jnp.int32, (SB_BQ - lo, SB_BK), 1) + kj * SB_BK
            mask = kpos < qpos
        acc = acc_ref[lo:, :]
        for a in range(heads):
            z = _dot(jnp.where(lane == a, q, 0.0), k) + bias_ref[hp * heads + a] * LOG2E
            csum = run_ref[a, lo:] + _sb_mass(z, mask, u)
            acc = acc + _dot_nt(_sb_weights(z, mask, csum), jnp.where(vrow == a, v, 0.0))
            run_ref[a, lo:] = csum[:, 0:1]
        acc_ref[lo:, :] = acc

    for c in range(ratio):
        @pl.when(kj == qi * ratio + c)
        def _():
            step(True, c * SB_BK)

    @pl.when(jnp.logical_not(diag))
    def _():
        step(False, 0)

    @pl.when(kj == 0)
    def _():
        o_ref[...] = acc_ref[...]


def _sb_prompt(q, k, v, bias, bsz, t_len):
    d = H_B * HD_B
    nq = t_len // SB_BQ
    ratio = SB_BQ // SB_BK
    pairs = [(qi, kj) for qi in range(nq) for kj in range(qi * ratio + ratio - 1, -1, -1)]
    qt = jnp.array([pq for pq, _ in pairs], jnp.int32)
    kt = jnp.array([pk for _, pk in pairs], jnp.int32)
    ii = lax.broadcasted_iota(jnp.int32, (SB_BK, SB_BK), 0)
    jj = lax.broadcasted_iota(jnp.int32, (SB_BK, SB_BK), 1)
    u = (ii >= jj).astype(BF16)
    heads = SB_W // HD_B
    grid_spec = pltpu.PrefetchScalarGridSpec(
        num_scalar_prefetch=2,
        grid=(bsz, d // SB_W, len(pairs)),
        in_specs=[
            pl.BlockSpec(memory_space=pltpu.SMEM),
            pl.BlockSpec((SB_BQ, SB_W), lambda b, h, p, qt, kt: (b * nq + qt[p], h)),
            pl.BlockSpec((None, SB_W, SB_BK), lambda b, h, p, qt, kt: (b, h, kt[p])),
            pl.BlockSpec((None, SB_W, SB_BK), lambda b, h, p, qt, kt: (b, h, kt[p])),
            pl.BlockSpec((SB_BK, SB_BK), lambda b, h, p, qt, kt: (0, 0)),
        ],
        out_specs=pl.BlockSpec((SB_BQ, SB_W), lambda b, h, p, qt, kt: (b * nq + qt[p], h)),
        scratch_shapes=[pltpu.VMEM((SB_BQ, SB_W), F32), pltpu.VMEM((heads, SB_BQ, 1), F32)],
    )
    return pl.pallas_call(
        _sb_prompt_body,
        grid_spec=grid_spec,
        out_shape=jax.ShapeDtypeStruct((bsz * t_len, d), F32),
        compiler_params=_cparams(("parallel", "parallel", "arbitrary")),
        name="sb_prompt",
    )(qt, kt, bias, q, k, v, u)


def _sb_sample_body(pt_ref, bias_ref, q_ref, kn_ref, vn_ref, *rest, t_len):
    kp_refs = rest[:SB_PAGES]
    vp_refs = rest[SB_PAGES:2 * SB_PAGES]
    u_ref, o_ref, qbd_ref, acc_ref, run_ref, bcol_ref = rest[2 * SB_PAGES:]
    s = pl.program_id(1)
    d = H_B * HD_B
    rows = H_B * t_len
    rowh = _div(lax.broadcasted_iota(jnp.int32, (rows, 1), 0), t_len)
    colh = _div(lax.broadcasted_iota(jnp.int32, (1, d), 1), HD_B)
    u = u_ref[...]

    def blocks(kbs, vbs, masks):
        zs = [(_dot_nt(qbd_ref[...], kb()) if m is not None else _dot(qbd_ref[...], kb())) + bcol_ref[...]
              for kb, m in zip(kbs, masks)]
        masses = [_sb_mass(z, m, u) for z, m in zip(zs, masks)]
        run = run_ref[...]
        pv = acc_ref[...]
        for z, m, mass, vb in zip(zs, masks, masses, vbs):
            csum = run + mass
            att = _sb_weights(z, m, csum)
            pv = pv + (_dot(att, vb()) if m is not None else _dot_nt(att, vb()))
            run = csum[:, 0:1]
        acc_ref[...] = pv
        run_ref[...] = run

    pages_k = [functools.partial(lambda r: r[0], r) for r in kp_refs]
    pages_v = [functools.partial(lambda r: r[0], r) for r in vp_refs]

    @pl.when(s == 0)
    def _():
        q = q_ref[0] * SB_QSCALE
        qbd_ref[...] = jnp.where(rowh == colh, jnp.concatenate([q] * H_B, axis=0), 0.0).astype(BF16)
        bcol = jnp.zeros((rows, 1), F32)
        for h in range(H_B):
            bcol = jnp.where(rowh == h, bias_ref[h] * LOG2E, bcol)
        bcol_ref[...] = bcol
        acc_ref[...] = jnp.zeros_like(acc_ref)
        run_ref[...] = jnp.zeros_like(run_ref)
        pad = jnp.zeros((PAGE - t_len, d), F32)
        t = lax.broadcasted_iota(jnp.int32, (rows, PAGE), 0) & (t_len - 1)
        jk = lax.broadcasted_iota(jnp.int32, (rows, PAGE), 1)
        blocks([lambda: jnp.concatenate([kn_ref[0], pad], axis=0)] + pages_k,
               [lambda: jnp.concatenate([vn_ref[0], pad], axis=0)] + pages_v,
               [jk < t] + [None] * SB_PAGES)

    @pl.when(s != 0)
    def _():
        blocks(pages_k, pages_v, [None] * SB_PAGES)

    @pl.when(s == pl.num_programs(1) - 1)
    def _():
        g = jnp.where(rowh == colh, acc_ref[...], 0.0)
        o_ref[0] = jnp.sum(g.reshape(H_B, t_len, d), axis=0)


def _sb_sample(q, k_new, v_new, cache_k, cache_v, page_table, bias, bsz, t_len):
    d = H_B * HD_B
    n_pages = page_table.shape[1]
    rows = H_B * t_len
    ii = lax.broadcasted_iota(jnp.int32, (PAGE, PAGE), 0)
    jj = lax.broadcasted_iota(jnp.int32, (PAGE, PAGE), 1)
    u = (ii >= jj).astype(BF16)
    tok = lambda b, s, pt: (b, 0, 0)

    def page(p):
        return lambda b, s, pt: (pt[b * n_pages + (n_pages - 1 - (s * SB_PAGES + p))], 0, 0)

    page_specs = [pl.BlockSpec((1, d, PAGE), page(p)) for p in range(SB_PAGES)]
    grid_spec = pltpu.PrefetchScalarGridSpec(
        num_scalar_prefetch=1,
        grid=(bsz, n_pages // SB_PAGES),
        in_specs=[
            pl.BlockSpec(memory_space=pltpu.SMEM),
            pl.BlockSpec((1, t_len, d), tok),
            pl.BlockSpec((1, t_len, d), tok),
            pl.BlockSpec((1, t_len, d), tok),
            *page_specs,
            *page_specs,
            pl.BlockSpec((PAGE, PAGE), lambda b, s, pt: (0, 0)),
        ],
        out_specs=pl.BlockSpec((1, t_len, d), tok),
        scratch_shapes=[
            pltpu.VMEM((rows, d), BF16),
            pltpu.VMEM((rows, d), F32),
            pltpu.VMEM((rows, 1), F32),
            pltpu.VMEM((rows, 1), F32),
        ],
    )
    out = pl.pallas_call(
        functools.partial(_sb_sample_body, t_len=t_len),
        grid_spec=grid_spec,
        out_shape=jax.ShapeDtypeStruct((bsz, t_len, d), F32),
        compiler_params=_cparams(("parallel", "arbitrary")),
        name="sb_sample",
    )(page_table.reshape(-1), bias, q.reshape(bsz, t_len, d), k_new.reshape(bsz, t_len, d),
      v_new.reshape(bsz, t_len, d), *([cache_k] * SB_PAGES), *([cache_v] * SB_PAGES), u)
    return out.reshape(bsz * t_len, d)


def kernel(x_prompt, x_sample, cache_k, cache_v, state_conv, state_rec, page_table, p_prompt, p_sample, ffn1_norm, ffn1_w_in, ffn1_w_out, mix_norm, ffn2_norm, ffn2_w_in, ffn2_w_out, ple_w_in, ple_norm, ple_w_gate, a_w_in, a_conv, a_A_log, a_dt_bias, a_o_norm, a_w_out, kv_norm, w_kv, b_w_q, b_bias, b_w_o, final_norm):
    bp, tp, d = x_prompt.shape
    bs, ts, _ = x_sample.shape
    depth = ffn1_norm.shape[0]
    n_a = a_w_in.shape[0]
    mp = bp * tp
    ms = bs * ts
    dkv = H_B * HD_B
    n_main = QKV_DIM + H_A * DV

    x = jnp.concatenate([x_prompt.reshape(mp, d), x_sample.reshape(ms, d)], axis=0)
    p_all = jnp.concatenate([p_prompt.reshape(depth, mp, -1), p_sample.reshape(depth, ms, -1)], axis=1)
    ck = cache_k.transpose(0, 2, 3, 1).reshape(cache_k.shape[0], dkv, cache_k.shape[1])
    cv = cache_v.transpose(0, 2, 3, 1).reshape(cache_v.shape[0], dkv, cache_v.shape[1])
    prev8 = jnp.pad(state_conv, ((0, 0), (0, 0), (SUBLANES - (CONV_W - 1), 0), (0, 0)))

    convs_p, convs_s, recs_p, recs_s = [], [], [], []
    for i in range(depth):
        x = _ffn(x, ffn1_norm[i], ffn1_w_in[i].astype(BF16), ffn1_w_out[i].astype(BF16))
        if i < n_a:
            proj, gates, gates_t = _gdn_proj(
                x, mix_norm[i], a_w_in[i][:, :n_main].astype(BF16), a_w_in[i][:, n_main:],
                a_A_log[i], a_dt_bias[i])
            gt = gates_t.reshape(2 * H_A, -1, CHUNK).transpose(1, 0, 2)
            o_p, s_p = _gdn_prompt(proj, gates, gt, a_conv[i], a_o_norm[i], bp, tp)
            o_s, s_s = _gdn_sample(proj, gates, gt, a_conv[i], a_o_norm[i], prev8[i], state_rec, i,
                                   mp, bs, ts)
            convs_p.append(jnp.stack([proj[(b + 1) * tp - (CONV_W - 1):(b + 1) * tp, :QKV_DIM]
                                      for b in range(bp)]))
            convs_s.append(proj[mp:, :QKV_DIM].reshape(bs, ts, QKV_DIM)[:, ts - (CONV_W - 1):])
            recs_p.append(s_p)
            recs_s.append(s_s)
            w_mix = a_w_out[i].astype(BF16)
        else:
            ib = i - n_a
            q = _norm_matmul(x, mix_norm[i], b_w_q[ib].astype(BF16))
            o_p = _sb_prompt(q, kt_p, vt_p, b_bias[ib], bp, tp)
            o_s = _sb_sample(q[mp:], k_s, v_s, ck, cv, page_table, b_bias[ib], bs, ts)
            w_mix = b_w_o[ib].astype(BF16)
        ffn2 = (ffn2_norm[i], ffn2_w_in[i].astype(BF16), ffn2_w_out[i].astype(BF16))
        ple = (p_all[i], ple_norm[i], ple_w_in[i].astype(BF16), ple_w_gate[i].astype(BF16))
        if i < depth - 1:
            x = _ffn_tail(x, o_p, o_s, w_mix, *ffn2, ple)
        else:
            x = _ffn_tail(x, o_p, o_s, w_mix, *ffn2)
            y_p, y_s = _ple_final(x, *ple, final_norm, mp)
        if i == n_a - 1:
            wk = w_kv[:, :dkv].astype(BF16)
            wv = w_kv[:, dkv:].astype(BF16)
            kt_p, vt_p = _norm_matmul_t(x, kv_norm, wk.T, wv.T, bp, tp)
            k_s = _norm_matmul(x[mp:], kv_norm, wk)
            v_s = _norm_matmul(x[mp:], kv_norm, wv)

    y_prompt = y_p.reshape(bp, tp, d)
    y_sample = y_s.reshape(bs, ts, d)
    k_prompt = kt_p.reshape(bp, H_B, HD_B, tp).transpose(0, 3, 1, 2)
    v_prompt = vt_p.reshape(bp, H_B, HD_B, tp).transpose(0, 3, 1, 2)
    k_sample = k_s.reshape(bs, ts, H_B, HD_B)
    v_sample = v_s.reshape(bs, ts, H_B, HD_B)
    return (y_prompt, y_sample, jnp.stack(convs_p), jnp.stack(recs_p), k_prompt, v_prompt,
            jnp.stack(convs_s), jnp.stack(recs_s), k_sample, v_sample)
```
